```python
import jax
import jax.numpy as jnp
from jax import lax
import numpy as np

D_MODEL = 1024
BATCH = 8
SEQ = 2048
DEPTH = 1
DEC_BATCH = 128
DEC_SEQ = 4
PAST_LEN = 8192
PAGE_SIZE = 128

HEAD_DIM = 64
MIX_W = D_MODEL
ATT_W = MIX_W // 2
RWKV_W = MIX_W - ATT_W
N_ATT_HEADS = ATT_W // HEAD_DIM
N_RWKV_HEADS = RWKV_W // HEAD_DIM
DECAY_LORA = 64
AAA_LORA = 64
GATE_LORA = 128
RWKV_PROJ_W = 3 * RWKV_W + DECAY_LORA + AAA_LORA + GATE_LORA
PROJ_W = 3 * ATT_W + RWKV_PROJ_W
D_FF = -(-8 * D_MODEL // (3 * 256)) * 256
Q_BLOCK = 128
SB_SCALE = HEAD_DIM ** -0.5
NORM_EPS = 1e-6
GN_EPS = 64e-5

kernel_name = "hymba_stickbreak_rwkv7_step"


def rmsnorm(x, g, eps=NORM_EPS):
    xf = x.astype(jnp.float32)
    y = xf * lax.rsqrt(jnp.mean(xf * xf, axis=-1, keepdims=True) + eps) * g.astype(jnp.float32)
    return y.astype(x.dtype)


def sb_block(q, k, v, bias, mask, log_carry):
    z = (jnp.einsum('bqhd,bkhd->bhqk', q, k, preferred_element_type=jnp.float32) * SB_SCALE
         + bias.astype(jnp.float32)[None, :, None, None])
    log_keep = jnp.where(mask, jax.nn.log_sigmoid(-z), 0.0)
    suffix = lax.cumsum(log_keep, axis=3, reverse=True) - log_keep + log_carry[..., None]
    weight = jnp.where(mask, jnp.exp(jax.nn.log_sigmoid(z) + suffix), 0.0)
    out = jnp.einsum('bhqk,bkhd->bhqd', weight, v.astype(jnp.float32))
    return out, log_carry + jnp.sum(log_keep, axis=3)


def sb_prompt(q, k, v, bias):
    B, S, H, D = q.shape
    nb = S // Q_BLOCK
    qb = jnp.moveaxis(q.reshape(B, nb, Q_BLOCK, H, D), 1, 0)
    kpos = jnp.arange(S)
    zeros = jnp.zeros((B, H, Q_BLOCK), jnp.float32)

    def one_block(args):
        qi, i = args
        qpos = i * Q_BLOCK + jnp.arange(Q_BLOCK)
        out, _ = sb_block(qi, k, v, bias, kpos[None, :] < qpos[:, None], zeros)
        return out

    o = lax.map(one_block, (qb, jnp.arange(nb)))
    return o.transpose(1, 0, 3, 2, 4).reshape(B, S, H * D)


def sb_decode(q, k_new, v_new, bias, cache_k, cache_v, page_table, layer):
    B, T, H, D = q.shape
    tpos = jnp.arange(T)
    out, carry = sb_block(q, k_new, v_new, bias, tpos[None, :] < tpos[:, None],
                          jnp.zeros((B, H, T), jnp.float32))
    full = jnp.ones((T, cache_k.shape[2]), bool)

    def page_step(c, pages):
        acc, lc = c
        o, lc = sb_block(q, cache_k[layer, pages], cache_v[layer, pages], bias, full, lc)
        return (acc + o, lc), None

    (out, _), _ = lax.scan(page_step, (out, carry), page_table.T[::-1])
    return out.transpose(0, 2, 1, 3).reshape(B, T, H * D)


def rwkv_recurrence(s0, r, w, k, v, kk, a):
    def step(S, xs):
        r_t, w_t, k_t, v_t, kk_t, a_t = xs
        S = (S * w_t[:, :, None, :]
             + jnp.einsum('bhvk,bhk->bhv', S, -kk_t)[..., None] * (kk_t * a_t)[:, :, None, :]
             + v_t[..., None] * k_t[:, :, None, :])
        return S, jnp.einsum('bhvk,bhk->bhv', S, r_t)

    xs = tuple(jnp.moveaxis(t, 1, 0) for t in (r, w, k, v, kk, a))
    S, o = lax.scan(step, s0, xs)
    return S, jnp.moveaxis(o, 0, 1)


def rwkv_mix(pr, shift_prev, s0, mu_shift, w0, w2, a0, a2, g2, k_k, k_a, r_k, gn_w, gn_b):
    B, T, _ = pr.shape
    f32 = jnp.float32
    prev = jnp.concatenate([shift_prev[:, None, :].astype(pr.dtype), pr[:, :-1]], axis=1)
    xs = pr + (prev - pr) * mu_shift
    o1 = RWKV_W
    o2 = 2 * RWKV_W
    o3 = 3 * RWKV_W
    o4 = o3 + DECAY_LORA
    o5 = o4 + AAA_LORA
    r, k, v, wd, ad, gd = jnp.split(xs, [o1, o2, o3, o4, o5], axis=-1)
    w_raw = -jax.nn.softplus(-(w0 + jnp.tanh(wd) @ w2).astype(f32)) - 0.5
    decay = jnp.exp(-jnp.exp(w_raw))
    a = jax.nn.sigmoid((a0 + ad @ a2).astype(f32))
    g = jax.nn.sigmoid(gd) @ g2

    def heads(t):
        return t.reshape(B, T, N_RWKV_HEADS, HEAD_DIM).astype(f32)

    kk = heads(k * k_k)
    kk = kk / jnp.maximum(jnp.sqrt(jnp.sum(kk * kk, axis=-1, keepdims=True)), 1e-12)
    k_mod = k.astype(f32) * (1.0 + (a - 1.0) * k_a)
    rh, kh, vh = heads(r), heads(k_mod), heads(v)
    S, o = rwkv_recurrence(s0.astype(f32), rh, heads(decay), kh, vh, kk, heads(a))
    mean = jnp.mean(o, axis=-1, keepdims=True)
    var = jnp.mean(jnp.square(o - mean), axis=-1, keepdims=True)
    o = ((o - mean) * lax.rsqrt(var + GN_EPS)).reshape(B, T, RWKV_W) * gn_w + gn_b
    bonus = jnp.sum(rh * kh * r_k.reshape(N_RWKV_HEADS, HEAD_DIM).astype(f32), axis=-1, keepdims=True) * vh
    o = (o + bonus.reshape(B, T, RWKV_W)) * g
    return o.astype(pr.dtype), S, pr[:, -1]


def layer_forward(x, attend, shift_prev, s0, w_in, q_norm, k_norm, sb_bias, mu_shift, w0, w2, a0, a2, g2,
                  k_k, k_a, r_k, gn_w, gn_b, w_o, norm1, norm2, w_gate, w_up, w_down):
    B, T, _ = x.shape
    xn = rmsnorm(x, norm1)
    p = jnp.einsum('btd,dp->btp', xn, w_in)
    pa, pr = p[..., :3 * ATT_W], p[..., 3 * ATT_W:]
    q, k, v = jnp.split(pa, 3, axis=-1)
    q = rmsnorm(q.reshape(B, T, N_ATT_HEADS, HEAD_DIM), q_norm)
    k = rmsnorm(k.reshape(B, T, N_ATT_HEADS, HEAD_DIM), k_norm)
    v = v.reshape(B, T, N_ATT_HEADS, HEAD_DIM)
    att = attend(q, k, v, sb_bias).astype(x.dtype)
    rw, S, shift_new = rwkv_mix(pr, shift_prev, s0, mu_shift, w0, w2, a0, a2, g2, k_k, k_a, r_k, gn_w, gn_b)
    h = x + jnp.concatenate([att, rw], axis=-1) @ w_o
    hn = rmsnorm(h, norm2)
    y = h + (jax.nn.silu(hn @ w_gate) * (hn @ w_up)) @ w_down
    return y, k, v, S, shift_new


def setup_inputs(seed: int = 0) -> dict:
    key = jax.random.key(seed)
    ks = iter(jax.random.split(key, 40))
    f32 = jnp.float32

    def nrm(shape, scale):
        return jax.random.normal(next(ks), shape, f32) * scale

    n_pages = PAST_LEN // PAGE_SIZE
    n_used = DEC_BATCH * n_pages
    n_pool = (5 * n_used) // 4
    page_table = jax.random.permutation(next(ks), n_pool)[:n_used].astype(jnp.int32).reshape(DEC_BATCH, n_pages)
    return {
        'x_prompt': nrm((BATCH, SEQ, D_MODEL), 1.0),
        'x_sample': nrm((DEC_BATCH, DEC_SEQ, D_MODEL), 1.0),
        'cache_k': nrm((DEPTH, n_pool, PAGE_SIZE, N_ATT_HEADS, HEAD_DIM), 1.0),
        'cache_v': nrm((DEPTH, n_pool, PAGE_SIZE, N_ATT_HEADS, HEAD_DIM), 1.0),
        'state_rwkv': nrm((DEPTH, DEC_BATCH, N_RWKV_HEADS, HEAD_DIM, HEAD_DIM), 1.0),
        'state_shift': nrm((DEPTH, DEC_BATCH, RWKV_PROJ_W), 1.0),
        'page_table': page_table,
        'w_in': nrm((DEPTH, D_MODEL, PROJ_W), D_MODEL ** -0.5),
        'q_norm': 1.0 + nrm((DEPTH, HEAD_DIM), 0.02),
        'k_norm': 1.0 + nrm((DEPTH, HEAD_DIM), 0.02),
        'sb_bias': jax.random.uniform(next(ks), (DEPTH, N_ATT_HEADS), f32, -7.0, -5.0),
        'mu_shift': jax.random.uniform(next(ks), (DEPTH, RWKV_PROJ_W), f32),
        'w0': jax.random.uniform(next(ks), (DEPTH, RWKV_W), f32, -4.0, -1.0),
        'w2': nrm((DEPTH, DECAY_LORA, RWKV_W), 0.1 * DECAY_LORA ** -0.5),
        'a0': nrm((DEPTH, RWKV_W), 0.1),
        'a2': nrm((DEPTH, AAA_LORA, RWKV_W), AAA_LORA ** -0.5),
        'g2': nrm((DEPTH, GATE_LORA, RWKV_W), GATE_LORA ** -0.5),
        'k_k': 0.85 + nrm((DEPTH, RWKV_W), 0.02),
        'k_a': 1.0 + nrm((DEPTH, RWKV_W), 0.02),
        'r_k': nrm((DEPTH, RWKV_W), 0.1),
        'gn_w': 1.0 + nrm((DEPTH, RWKV_W), 0.02),
        'gn_b': nrm((DEPTH, RWKV_W), 0.02),
        'w_o': nrm((DEPTH, MIX_W, D_MODEL), MIX_W ** -0.5),
        'norm1': 1.0 + nrm((DEPTH, D_MODEL), 0.02),
        'norm2': 1.0 + nrm((DEPTH, D_MODEL), 0.02),
        'w_gate': nrm((DEPTH, D_MODEL, D_FF), D_MODEL ** -0.5),
        'w_up': nrm((DEPTH, D_MODEL, D_FF), D_MODEL ** -0.5),
        'w_down': nrm((DEPTH, D_FF, D_MODEL), D_FF ** -0.5),
    }


def reference(x_prompt, x_sample, cache_k, cache_v, state_rwkv, state_shift, page_table,
              w_in, q_norm, k_norm, sb_bias, mu_shift, w0, w2, a0, a2, g2, k_k, k_a, r_k, gn_w, gn_b,
              w_o, norm1, norm2, w_gate, w_up, w_down):
    yp = x_prompt
    ys = x_sample
    bp = x_prompt.shape[0]
    kp_l, vp_l, sp_l, hp_l, ks_l, vs_l, ss_l, hs_l = [], [], [], [], [], [], [], []
    for l in range(DEPTH):
        lp = (w_in[l], q_norm[l], k_norm[l], sb_bias[l], mu_shift[l], w0[l], w2[l], a0[l], a2[l], g2[l],
              k_k[l], k_a[l], r_k[l], gn_w[l], gn_b[l], w_o[l], norm1[l], norm2[l],
              w_gate[l], w_up[l], w_down[l])
        yp, k_p, v_p, s_p, h_p = layer_forward(
            yp, sb_prompt, jnp.zeros((bp, RWKV_PROJ_W), jnp.float32),
            jnp.zeros((bp, N_RWKV_HEADS, HEAD_DIM, HEAD_DIM), jnp.float32), *lp)

        def attend_sample(q, k, v, bias, layer=l):
            return sb_decode(q, k, v, bias, cache_k, cache_v, page_table, layer)

        ys, k_s, v_s, s_s, h_s = layer_forward(ys, attend_sample, state_shift[l], state_rwkv[l], *lp)
        kp_l.append(k_p)
        vp_l.append(v_p)
        sp_l.append(s_p)
        hp_l.append(h_p)
        ks_l.append(k_s)
        vs_l.append(v_s)
        ss_l.append(s_s)
        hs_l.append(h_s)
    k_prompt = jnp.stack(kp_l)
    v_prompt = jnp.stack(vp_l)
    rwkv_prompt = jnp.stack(sp_l)
    shift_prompt = jnp.stack(hp_l)
    k_sample = jnp.stack(ks_l)
    v_sample = jnp.stack(vs_l)
    rwkv_sample = jnp.stack(ss_l)
    shift_sample = jnp.stack(hs_l)
    return (yp, ys, k_prompt, v_prompt, rwkv_prompt, shift_prompt, k_sample, v_sample, rwkv_sample, shift_sample)
```

```python
import functools

import jax
import jax.numpy as jnp
from jax import lax
from jax.experimental import pallas as pl
from jax.experimental.pallas import tpu as pltpu

F32 = jnp.float32
BF16 = jnp.bfloat16

HEAD_DIM = 64
N_HEADS = 8
GROUP_W = N_HEADS * HEAD_DIM
LORA_W = 256
RWKV_PROJ_W = 3 * GROUP_W + LORA_W
SB_SCALE = HEAD_DIM ** -0.5
NORM_EPS = 1e-6
GN_EPS = 64e-5
LANES = 128
VMEM_LIMIT = 56 * 1024 * 1024
TOKEN_TILE = 256
ATT_TILE = 256
PREP_CHUNK = 256
REC_CHUNK = 32
PAGES_PER_STEP = 8


def _dot(a, b):
    return jnp.dot(a, b, preferred_element_type=F32)


def _dot_nt(a, b):
    return lax.dot_general(a, b, (((1,), (1,)), ((), ())), preferred_element_type=F32)


def _softplus(y):
    return jnp.maximum(y, 0.0) + jnp.log(1.0 + jnp.exp(-jnp.abs(y)))


def _sigmoid(y):
    return 1.0 / (1.0 + jnp.exp(-y))


def _params(sem):
    return pltpu.CompilerParams(dimension_semantics=sem, vmem_limit_bytes=VMEM_LIMIT)


def _const_spec(shape):
    nd = len(shape)
    return pl.BlockSpec(shape, lambda *_: (0,) * nd)


def _inproj_body(x_ref, n1_ref, w_ref, qg_ref, kg_ref, gm_ref,
                 q_ref, k_ref, kb_ref, v_ref, vb_ref, pr_ref):
    x = x_ref[...]
    xn = x * lax.rsqrt(jnp.mean(x * x, axis=-1, keepdims=True) + NORM_EPS) * n1_ref[...]
    xb = xn.astype(BF16)

    def head_norm(t, g):
        ms = _dot((t * t).astype(BF16), gm_ref[...])
        return t * lax.rsqrt(ms + NORM_EPS) * g

    q = _dot(xb, w_ref[:, 0:GROUP_W])
    q_ref[...] = (head_norm(q, qg_ref[...]) * SB_SCALE).astype(BF16)
    k = head_norm(_dot(xb, w_ref[:, GROUP_W:2 * GROUP_W]), kg_ref[...])
    k_ref[...] = k
    kb_ref[...] = k.astype(BF16)
    v = _dot(xb, w_ref[:, 2 * GROUP_W:3 * GROUP_W])
    v_ref[...] = v
    vb_ref[...] = v.astype(BF16)
    pr_ref[...] = _dot(xb, w_ref[:, 3 * GROUP_W:])


def _inproj(x, n1, w_in_b, qg, kg, gmean, tm):
    n, d = x.shape
    pw = w_in_b.shape[1]
    tok = lambda w: pl.BlockSpec((tm, w), lambda i: (i, 0))
    return pl.pallas_call(
        _inproj_body,
        grid=(n // tm,),
        in_specs=[tok(d), _const_spec((1, d)), _const_spec((d, pw)),
                  _const_spec((1, GROUP_W)), _const_spec((1, GROUP_W)),
                  _const_spec((GROUP_W, GROUP_W))],
        out_specs=[tok(GROUP_W), tok(GROUP_W), tok(GROUP_W), tok(GROUP_W), tok(GROUP_W),
                   tok(RWKV_PROJ_W)],
        out_shape=[jax.ShapeDtypeStruct((n, GROUP_W), BF16),
                   jax.ShapeDtypeStruct((n, GROUP_W), F32),
                   jax.ShapeDtypeStruct((n, GROUP_W), BF16),
                   jax.ShapeDtypeStruct((n, GROUP_W), F32),
                   jax.ShapeDtypeStruct((n, GROUP_W), BF16),
                   jax.ShapeDtypeStruct((n, RWKV_PROJ_W), F32)],
        compiler_params=_params(("parallel",)),
        name="inproj",
    )(x, n1, w_in_b, qg, kg, gmean)


def _sb_step(z, v_b, u, carry, mask):
    sp = _softplus(z)
    log_keep = -sp
    if mask is not None:
        log_keep = jnp.where(mask, log_keep, 0.0)
    hi = log_keep.astype(BF16)
    lo = (log_keep - hi.astype(F32)).astype(BF16)
    suffix = _dot(hi, u) + _dot(lo, u)
    weight = jnp.exp((z - sp) + suffix + carry)
    if mask is not None:
        weight = jnp.where(mask, weight, 0.0)
    out = _dot(weight.astype(BF16), v_b)
    return out, carry + jnp.sum(log_keep, axis=1, keepdims=True)


def _suffix_matrix(tk):
    j = lax.broadcasted_iota(jnp.int32, (tk, tk), 0)
    s = lax.broadcasted_iota(jnp.int32, (tk, tk), 1)
    return (j > s).astype(BF16)


def _sb_prompt_body(bias_ref, q_ref, k_ref, v_ref, u_ref, o_ref, *, tq):
    hp = pl.program_id(1)
    qi = pl.program_id(2)
    u = u_ref[...]
    q = q_ref[0]
    lane_head = lax.broadcasted_iota(jnp.int32, (tq, LANES), 1) // HEAD_DIM
    row = lax.broadcasted_iota(jnp.int32, (tq, tq), 0)
    col = lax.broadcasted_iota(jnp.int32, (tq, tq), 1)
    causal = col < row
    accs = []
    for j in range(2):
        bias = bias_ref[hp * 2 + j]
        qh = jnp.where(lane_head == j, q, jnp.zeros_like(q))

        def block(kb, carry, acc, mask, qh=qh, bias=bias):
            start = pl.multiple_of(kb * tq, tq)
            kh = k_ref[0, pl.ds(start, tq), :]
            vh = v_ref[0, pl.ds(start, tq), :]
            z = _dot_nt(qh, kh) + bias
            out, carry = _sb_step(z, vh, u, carry, mask)
            return carry, acc + out

        carry, acc = block(qi, jnp.zeros((tq, 1), F32), jnp.zeros((tq, LANES), F32), causal)
        carry, acc = lax.fori_loop(
            0, qi, lambda i, c, block=block: block(qi - 1 - i, c[0], c[1], None), (carry, acc))
        accs.append(acc)
    o_ref[0] = jnp.where(lane_head == 0, accs[0], accs[1])


def _sb_prompt(q_b, k_b, v_b, bias, tq):
    b, s, _ = q_b.shape
    hpairs = GROUP_W // LANES
    return pl.pallas_call(
        functools.partial(_sb_prompt_body, tq=tq),
        grid=(b, hpairs, s // tq),
        in_specs=[pl.BlockSpec(memory_space=pltpu.SMEM),
                  pl.BlockSpec((1, tq, LANES), lambda bi, hp, qi: (bi, qi, hp)),
                  pl.BlockSpec((1, s, LANES), lambda bi, hp, qi: (bi, 0, hp)),
                  pl.BlockSpec((1, s, LANES), lambda bi, hp, qi: (bi, 0, hp)),
                  _const_spec((tq, tq))],
        out_specs=pl.BlockSpec((1, tq, LANES), lambda bi, hp, qi: (bi, qi, hp)),
        out_shape=jax.ShapeDtypeStruct((b, s, GROUP_W), F32),
        compiler_params=_params(("parallel", "parallel", "arbitrary")),
        name="sb_prompt",
    )(bias, q_b, k_b, v_b, _suffix_matrix(tq))


def _sb_decode_body(pt_ref, qbd_ref, bias_ref, knew_ref, vnew_ref, u_ref, *rest, pps, t_new):
    k_refs = rest[:pps]
    v_refs = rest[pps:2 * pps]
    o_ref, acc_ref, carry_ref = rest[2 * pps:]
    jj = pl.program_id(1)
    qbd = qbd_ref[0]
    bias = bias_ref[...]
    u = u_ref[...]
    m, page = bias.shape

    @pl.when(jj == 0)
    def _():
        pad = jnp.zeros((page - knew_ref.shape[1], GROUP_W), F32)
        kb = jnp.concatenate([knew_ref[0], pad], axis=0).astype(BF16)
        vb = jnp.concatenate([vnew_ref[0], pad], axis=0).astype(BF16)
        qpos = lax.broadcasted_iota(jnp.int32, (m, page), 0) // N_HEADS
        kpos = lax.broadcasted_iota(jnp.int32, (m, page), 1)
        z = _dot_nt(qbd, kb) + bias
        out, carry = _sb_step(z, vb, u, jnp.zeros((m, page), F32), kpos < qpos)
        acc_ref[...] = out
        carry_ref[...] = carry

    acc = acc_ref[...]
    carry = carry_ref[...]
    for i in range(pps):
        kb = k_refs[i][0].astype(BF16)
        vb = v_refs[i][0].astype(BF16)
        z = _dot_nt(qbd, kb) + bias
        out, carry = _sb_step(z, vb, u, carry, None)
        acc = acc + out
    acc_ref[...] = acc
    carry_ref[...] = carry

    @pl.when(jj == pl.num_programs(1) - 1)
    def _():
        row_head = lax.broadcasted_iota(jnp.int32, (m, GROUP_W), 0) % N_HEADS
        lane_head = lax.broadcasted_iota(jnp.int32, (m, GROUP_W), 1) // HEAD_DIM
        own = jnp.where(row_head == lane_head, acc, 0.0)
        o_ref[0] = jnp.sum(own.reshape(t_new, N_HEADS, GROUP_W), axis=1)


def _sb_decode(q_b, k_new, v_new, bias, cache_k, cache_v, page_table, pps):
    nb, t_new, _ = q_b.shape
    n_pages = page_table.shape[1]
    page = cache_k.shape[1]
    m = t_new * N_HEADS
    t_pad = -(-t_new // 8) * 8
    lane_head = jnp.arange(GROUP_W) // HEAD_DIM
    sel = (lane_head[None, :] == jnp.arange(N_HEADS)[:, None])
    qbd = jnp.where(sel[None, None], q_b[:, :, None, :], jnp.zeros((), BF16)).reshape(nb, m, GROUP_W)
    bias_tile = jnp.broadcast_to(jnp.tile(bias, t_new)[:, None], (m, page)).astype(F32)
    padn = ((0, 0), (0, t_pad - t_new), (0, 0))
    k_new = jnp.pad(k_new, padn)
    v_new = jnp.pad(v_new, padn)

    def page_spec(i):
        return pl.BlockSpec(
            (1, page, GROUP_W),
            lambda b, jj, pt, i=i: (pt[b * n_pages + (n_pages - 1 - (jj * pps + i))], 0, 0))

    seq = lambda r, w: pl.BlockSpec((1, r, w), lambda b, jj, pt: (b, 0, 0))
    grid_spec = pltpu.PrefetchScalarGridSpec(
        num_scalar_prefetch=1,
        grid=(nb, n_pages // pps),
        in_specs=[seq(m, GROUP_W),
                  pl.BlockSpec((m, page), lambda b, jj, pt: (0, 0)),
                  seq(t_pad, GROUP_W), seq(t_pad, GROUP_W),
                  pl.BlockSpec((page, page), lambda b, jj, pt: (0, 0))]
                 + [page_spec(i) for i in range(pps)] * 2,
        out_specs=seq(t_new, GROUP_W),
        scratch_shapes=[pltpu.VMEM((m, GROUP_W), F32), pltpu.VMEM((m, page), F32)],
    )
    return pl.pallas_call(
        functools.partial(_sb_decode_body, pps=pps, t_new=t_new),
        grid_spec=grid_spec,
        out_shape=jax.ShapeDtypeStruct((nb, t_new, GROUP_W), F32),
        compiler_params=_params(("parallel", "arbitrary")),
        name="sb_decode",
    )(page_table.reshape(-1), qbd, bias_tile, k_new, v_new, _suffix_matrix(page),
      *([cache_k] * pps), *([cache_v] * pps))


def _rwkv_prep_body(pr_ref, sh_ref, mu_ref, w0_ref, a0_ref, w2_ref, a2_ref, g2_ref,
                    kk_ref, ka_ref, rk_ref, gs_ref,
                    w_o, nkk_o, b_o, km_o, r_o, v_o, g_o, bonus_o, prev_ref, *, seq_len):
    pr = pr_ref[0]
    tc = pr.shape[0]
    row = lax.broadcasted_iota(jnp.int32, pr.shape, 0)
    rolled = pltpu.roll(pr, 1, axis=0)
    if seq_len is None:
        @pl.when(pl.program_id(1) == 0)
        def _():
            prev_ref[...] = sh_ref[0]

        prev = jnp.where(row == 0, prev_ref[...], rolled)
        prev_ref[...] = pr[tc - 1:tc, :]
    else:
        prev = jnp.where(row % seq_len == 0, sh_ref[0], rolled)
    xs = pr + (prev - pr) * mu_ref[...]
    r = xs[:, 0:GROUP_W]
    k = xs[:, GROUP_W:2 * GROUP_W]
    v = xs[:, 2 * GROUP_W:3 * GROUP_W]
    wa = xs[:, 3 * GROUP_W:3 * GROUP_W + LANES]
    gd = xs[:, 3 * GROUP_W + LANES:]
    gs = gs_ref[...]
    w_raw = -_softplus(-(w0_ref[...] + _dot(jnp.tanh(wa).astype(BF16), w2_ref[...]))) - 0.5
    w_o[0] = jnp.exp(-jnp.exp(w_raw))
    a = _sigmoid(a0_ref[...] + _dot(wa.astype(BF16), a2_ref[...]))
    g_o[0] = _dot(_sigmoid(gd).astype(BF16), g2_ref[...])
    kk = k * kk_ref[...]
    ss = _dot((kk * kk).astype(BF16), gs)
    kk = kk / jnp.maximum(jnp.sqrt(ss), 1e-12)
    nkk_o[0] = -kk
    b_o[0] = kk * a
    km = k * (1.0 + (a - 1.0) * ka_ref[...])
    km_o[0] = km
    r_o[0] = r
    v_o[0] = v
    bonus_o[0] = _dot((r * km * rk_ref[...]).astype(BF16), gs) * v


def _rwkv_prep(pr, shift, seq_len, mu, w0, a0, w2p, a2p, g2b, k_k, k_a, r_k, gsum, tc):
    b, s, pw = pr.shape
    blk = pl.BlockSpec((1, tc, GROUP_W), lambda bi, c: (bi, c, 0))
    vec = _const_spec((1, GROUP_W))
    if seq_len is None:
        shift_spec = pl.BlockSpec((1, 1, pw), lambda bi, c: (bi, 0, 0))
    else:
        assert tc % seq_len == 0
        shift_spec = pl.BlockSpec((1, tc, pw), lambda bi, c: (bi, c, 0))
    return pl.pallas_call(
        functools.partial(_rwkv_prep_body, seq_len=seq_len),
        grid=(b, s // tc),
        in_specs=[pl.BlockSpec((1, tc, pw), lambda bi, c: (bi, c, 0)),
                  shift_spec,
                  _const_spec((1, pw)), vec, vec,
                  _const_spec((LANES, GROUP_W)), _const_spec((LANES, GROUP_W)),
                  _const_spec((LANES, GROUP_W)), vec, vec, vec,
                  _const_spec((GROUP_W, GROUP_W))],
        out_specs=[blk] * 8,
        out_shape=[jax.ShapeDtypeStruct((b, s, GROUP_W), F32)] * 8,
        scratch_shapes=[pltpu.VMEM((1, pw), F32)],
        compiler_params=_params(("parallel", "arbitrary")),
        name="rwkv_prep",
    )(pr, shift, mu, w0, a0, w2p, a2p, g2b, k_k, k_a, r_k, gsum)


def _rwkv_rec_body(w_ref, nkk_ref, b_ref, km_ref, r_ref, v_ref, s0_ref, o_ref, st_ref, s_ref,
                   *, tc, nv):
    c = pl.program_id(1)

    @pl.when(c == 0)
    def _():
        s_ref[...] = s0_ref[...]

    def step(t, carry):
        w = w_ref[t]
        nkk = nkk_ref[t]
        b = b_ref[t]
        km = km_ref[t]
        r = r_ref[t]
        for vi in range(nv):
            s = s_ref[vi]
            sa = jnp.sum(s * nkk, axis=0, keepdims=True)
            s = s * w + sa * b + v_ref[t, pl.ds(vi, 1), :] * km
            s_ref[vi] = s
            o_ref[t, pl.ds(vi, 1), :] = jnp.sum(s * r, axis=0, keepdims=True)
        return carry

    lax.fori_loop(0, tc, step, 0)

    @pl.when(c == pl.num_programs(1) - 1)
    def _():
        st_ref[...] = s_ref[...]


def _rwkv_rec(w, nkk, b, km, r, v, s0, tc):
    t, _, l = w.shape
    nv = v.shape[1]
    kvec = pl.BlockSpec((tc, HEAD_DIM, LANES), lambda g, c: (c, 0, g))
    vvec = pl.BlockSpec((tc, nv, LANES), lambda g, c: (c, 0, g))
    st = pl.BlockSpec((nv, HEAD_DIM, LANES), lambda g, c: (0, 0, g))
    return pl.pallas_call(
        functools.partial(_rwkv_rec_body, tc=tc, nv=nv),
        grid=(l // LANES, t // tc),
        in_specs=[kvec] * 5 + [vvec, st],
        out_specs=[vvec, st],
        out_shape=[jax.ShapeDtypeStruct((t, nv, l), F32),
                   jax.ShapeDtypeStruct((nv, HEAD_DIM, l), F32)],
        scratch_shapes=[pltpu.VMEM((nv, HEAD_DIM, LANES), F32)],
        compiler_params=_params(("parallel", "arbitrary")),
        name="rwkv_rec",
    )(w, nkk, b, km, r, v, s0)


def _out_ffn_body(x_ref, att_ref, o_ref, g_ref, bonus_ref, gnw_ref, gnb_ref, gm_ref,
                  wo_ref, n2_ref, wg_ref, wu_ref, wd_ref, y_ref):
    gm = gm_ref[...]
    o = o_ref[...]
    mean = _dot(o.astype(BF16), gm)
    d = o - mean
    var = _dot((d * d).astype(BF16), gm)
    rw = (d * lax.rsqrt(var + GN_EPS) * gnw_ref[...] + gnb_ref[...] + bonus_ref[...]) * g_ref[...]
    h = (x_ref[...] + _dot(att_ref[...].astype(BF16), wo_ref[0:GROUP_W, :])
         + _dot(rw.astype(BF16), wo_ref[GROUP_W:, :]))
    hn = h * lax.rsqrt(jnp.mean(h * h, axis=-1, keepdims=True) + NORM_EPS) * n2_ref[...]
    hb = hn.astype(BF16)
    gate = _dot(hb, wg_ref[...])
    act = gate * _sigmoid(gate) * _dot(hb, wu_ref[...])
    y_ref[...] = h + _dot(act.astype(BF16), wd_ref[...])


def _out_ffn(x, att, o, g, bonus, gn_w, gn_b, gmean, wo_b, n2, wg_b, wu_b, wd_b, tm):
    n, d = x.shape
    dff = wg_b.shape[1]
    tok = lambda w: pl.BlockSpec((tm, w), lambda i: (i, 0))
    once = lambda shape: pl.BlockSpec(shape, lambda i: (0, 0), pipeline_mode=pl.Buffered(1))
    return pl.pallas_call(
        _out_ffn_body,
        grid=(n // tm,),
        in_specs=[tok(d), tok(GROUP_W), tok(GROUP_W), tok(GROUP_W), tok(GROUP_W),
                  once((1, GROUP_W)), once((1, GROUP_W)), once((GROUP_W, GROUP_W)),
                  once((2 * GROUP_W, d)), once((1, d)),
                  once((d, dff)), once((d, dff)), once((dff, d))],
        out_specs=tok(d),
        out_shape=jax.ShapeDtypeStruct((n, d), F32),
        compiler_params=_params(("parallel",)),
        name="out_ffn",
    )(x, att, o, g, bonus, gn_w, gn_b, gmean, wo_b, n2, wg_b, wu_b, wd_b)


def _to_chain_lanes(x, split_v):
    b, t, _ = x.shape
    if split_v:
        y = x.reshape(b, t, N_HEADS, 2, HEAD_DIM // 2).transpose(1, 4, 3, 0, 2)
        return y.reshape(t, HEAD_DIM // 2, 2 * b * N_HEADS)
    y = x.reshape(b, t, N_HEADS, HEAD_DIM).transpose(1, 3, 0, 2)
    return y.reshape(t, HEAD_DIM, b * N_HEADS)


def _from_chain_lanes(o, b, split_v):
    t = o.shape[0]
    if split_v:
        y = o.reshape(t, HEAD_DIM // 2, 2, b, N_HEADS).transpose(3, 0, 4, 2, 1)
    else:
        y = o.reshape(t, HEAD_DIM, b, N_HEADS).transpose(2, 0, 3, 1)
    return y.reshape(b, t, GROUP_W)


def _rwkv_group(pr, shift_prev, s0, prm, tc_prep, tc_rec):
    b, t, pw = pr.shape
    weights = (prm["mu"], prm["w0"], prm["a0"], prm["w2p"], prm["a2p"], prm["g2b"],
               prm["k_k"], prm["k_a"], prm["r_k"], prm["gsum"])
    if t >= tc_prep:
        vecs = _rwkv_prep(pr, shift_prev[:, None, :], None, *weights, tc_prep)
    else:
        starts = jnp.zeros((b, t, pw), F32).at[:, 0, :].set(shift_prev)
        vecs = _rwkv_prep(pr.reshape(1, b * t, pw), starts.reshape(1, b * t, pw), t,
                          *weights, min(tc_prep, b * t))
        vecs = [a.reshape(b, t, GROUP_W) for a in vecs]
    w, nkk, bv, km, r, v, g, bonus = vecs
    chains = b * N_HEADS
    split_v = chains < LANES
    kvecs = [_to_chain_lanes(a, False) for a in (w, nkk, bv, km, r)]
    if split_v:
        kvecs = [jnp.concatenate([a, a], axis=-1) for a in kvecs]
    vv = _to_chain_lanes(v, split_v)
    nv = vv.shape[1]
    if s0 is None:
        s0c = jnp.zeros((nv, HEAD_DIM, vv.shape[2]), F32)
    else:
        s0c = s0.transpose(2, 3, 0, 1).reshape(HEAD_DIM, HEAD_DIM, chains)
    o, sT = _rwkv_rec(*kvecs, vv, s0c, tc_rec)
    o = _from_chain_lanes(o, b, split_v)
    if split_v:
        sT = sT.reshape(nv, HEAD_DIM, 2, b, N_HEADS).transpose(3, 4, 2, 0, 1)
    else:
        sT = sT.reshape(nv, HEAD_DIM, b, N_HEADS).transpose(2, 3, 0, 1)
    return o, g, bonus, sT.reshape(b, N_HEADS, HEAD_DIM, HEAD_DIM)


def _group_matrix(scale):
    i = jnp.arange(GROUP_W) // HEAD_DIM
    return jnp.where(i[:, None] == i[None, :], scale, 0.0).astype(BF16)


def kernel(x_prompt, x_sample, cache_k, cache_v, state_rwkv, state_shift, page_table, w_in, q_norm, k_norm, sb_bias, mu_shift, w0, w2, a0, a2, g2, k_k, k_a, r_k, gn_w, gn_b, w_o, norm1, norm2, w_gate, w_up, w_down):
    depth = w_in.shape[0]
    bp, sp, d = x_prompt.shape
    bs, ts, _ = x_sample.shape
    gmean = _group_matrix(1.0 / HEAD_DIM)
    gsum = _group_matrix(1.0)
    yp = x_prompt.reshape(bp * sp, d)
    ys = x_sample.reshape(bs * ts, d)
    outs = [[] for _ in range(8)]
    for l in range(depth):
        row = lambda a: a[l][None, :]
        w_in_b = w_in[l].astype(BF16)
        qg = jnp.tile(q_norm[l], N_HEADS)[None, :]
        kg = jnp.tile(k_norm[l], N_HEADS)[None, :]
        lora_pad = jnp.zeros((LANES // 2, GROUP_W), F32)
        prm = dict(mu=row(mu_shift), w0=row(w0), a0=row(a0),
                   w2p=jnp.concatenate([w2[l], lora_pad], axis=0).astype(BF16),
                   a2p=jnp.concatenate([lora_pad, a2[l]], axis=0).astype(BF16),
                   g2b=g2[l].astype(BF16), k_k=row(k_k), k_a=row(k_a), r_k=row(r_k), gsum=gsum)
        ffn = (row(gn_w), row(gn_b), gmean, w_o[l].astype(BF16), row(norm2),
               w_gate[l].astype(BF16), w_up[l].astype(BF16), w_down[l].astype(BF16))

        tm = min(TOKEN_TILE, bp * sp)
        q_b, k_p, k_b, v_p, v_b, pr = _inproj(yp, row(norm1), w_in_b, qg, kg, gmean, tm)
        seq = lambda a: a.reshape(bp, sp, a.shape[-1])
        att = _sb_prompt(seq(q_b), seq(k_b), seq(v_b), sb_bias[l], min(ATT_TILE, sp))
        pr = seq(pr)
        o, g, bonus, s_p = _rwkv_group(pr, jnp.zeros((bp, RWKV_PROJ_W), F32), None, prm,
                                       PREP_CHUNK, min(REC_CHUNK, sp))
        flat = lambda a: a.reshape(bp * sp, GROUP_W)
        yp = _out_ffn(yp, flat(att), flat(o), flat(g), flat(bonus), *ffn, tm)
        outs[0].append(k_p.reshape(bp, sp, N_HEADS, HEAD_DIM))
        outs[1].append(v_p.reshape(bp, sp, N_HEADS, HEAD_DIM))
        outs[2].append(s_p)
        outs[3].append(pr[:, -1])

        tm = min(TOKEN_TILE, bs * ts)
        q_b, k_s, _, v_s, _, pr = _inproj(ys, row(norm1), w_in_b, qg, kg, gmean, tm)
        seq = lambda a: a.reshape(bs, ts, a.shape[-1])
        n_pool, page = cache_k.shape[1], cache_k.shape[2]
        att = _sb_decode(seq(q_b), seq(k_s), seq(v_s), sb_bias[l],
                         cache_k[l].reshape(n_pool, page, GROUP_W),
                         cache_v[l].reshape(n_pool, page, GROUP_W), page_table,
                         min(PAGES_PER_STEP, page_table.shape[1]))
        pr = seq(pr)
        o, g, bonus, s_s = _rwkv_group(pr, state_shift[l], state_rwkv[l], prm,
                                       PREP_CHUNK, min(REC_CHUNK, ts))
        flat = lambda a: a.reshape(bs * ts, GROUP_W)
        ys = _out_ffn(ys, flat(att), flat(o), flat(g), flat(bonus), *ffn, tm)
        outs[4].append(k_s.reshape(bs, ts, N_HEADS, HEAD_DIM))
        outs[5].append(v_s.reshape(bs, ts, N_HEADS, HEAD_DIM))
        outs[6].append(s_s)
        outs[7].append(pr[:, -1])
    stacked = [jnp.stack(o) for o in outs]
    return (yp.reshape(bp, sp, d), ys.reshape(bs, ts, d), *stacked)
```

```python
import functools

import jax
import jax.numpy as jnp
from jax import lax
from jax.experimental import pallas as pl
from jax.experimental.pallas import tpu as pltpu

F32 = jnp.float32
BF16 = jnp.bfloat16

HEAD_DIM = 64
N_HEADS = 8
GROUP_W = N_HEADS * HEAD_DIM
LORA_W = 256
RWKV_PROJ_W = 3 * GROUP_W + LORA_W
SB_SCALE = HEAD_DIM ** -0.5
NORM_EPS = 1e-6
GN_EPS = 64e-5
LANES = 128
VMEM_LIMIT = 56 * 1024 * 1024
TOKEN_TILE = 256
ATT_TILE = 256
PREP_CHUNK = 256
REC_CHUNK = 32
SPREAD_UNROLL = 8
PAGES_PER_STEP = 8


def _dot(a, b):
    return jnp.dot(a, b, preferred_element_type=F32)


def _dot_nt(a, b):
    return lax.dot_general(a, b, (((1,), (1,)), ((), ())), preferred_element_type=F32)


def _softplus(y):
    return jnp.maximum(y, 0.0) + jnp.log(1.0 + jnp.exp(-jnp.abs(y)))


def _sigmoid(y):
    return 1.0 / (1.0 + jnp.exp(-y))


def _params(sem):
    return pltpu.CompilerParams(dimension_semantics=sem, vmem_limit_bytes=VMEM_LIMIT)


def _const_spec(shape):
    nd = len(shape)
    return pl.BlockSpec(shape, lambda *_: (0,) * nd)


def _inproj_body(x_ref, n1_ref, w_ref, qg_ref, kg_ref, gm_ref,
                 q_ref, k_ref, kb_ref, v_ref, vb_ref, pr_ref):
    x = x_ref[...]
    xn = x * lax.rsqrt(jnp.mean(x * x, axis=-1, keepdims=True) + NORM_EPS) * n1_ref[...]
    xb = xn.astype(BF16)

    def head_norm(t, g):
        ms = _dot((t * t).astype(BF16), gm_ref[...])
        return t * lax.rsqrt(ms + NORM_EPS) * g

    q = _dot(xb, w_ref[:, 0:GROUP_W])
    q_ref[...] = (head_norm(q, qg_ref[...]) * SB_SCALE).astype(BF16)
    k = head_norm(_dot(xb, w_ref[:, GROUP_W:2 * GROUP_W]), kg_ref[...])
    k_ref[...] = k
    kb_ref[...] = k.astype(BF16)
    v = _dot(xb, w_ref[:, 2 * GROUP_W:3 * GROUP_W])
    v_ref[...] = v
    vb_ref[...] = v.astype(BF16)
    pr_ref[...] = _dot(xb, w_ref[:, 3 * GROUP_W:])


def _inproj(x, n1, w_in_b, qg, kg, gmean, tm):
    n, d = x.shape
    pw = w_in_b.shape[1]
    tok = lambda w: pl.BlockSpec((tm, w), lambda i: (i, 0))
    return pl.pallas_call(
        _inproj_body,
        grid=(n // tm,),
        in_specs=[tok(d), _const_spec((1, d)), _const_spec((d, pw)),
                  _const_spec((1, GROUP_W)), _const_spec((1, GROUP_W)),
                  _const_spec((GROUP_W, GROUP_W))],
        out_specs=[tok(GROUP_W), tok(GROUP_W), tok(GROUP_W), tok(GROUP_W), tok(GROUP_W),
                   tok(RWKV_PROJ_W)],
        out_shape=[jax.ShapeDtypeStruct((n, GROUP_W), BF16),
                   jax.ShapeDtypeStruct((n, GROUP_W), F32),
                   jax.ShapeDtypeStruct((n, GROUP_W), BF16),
                   jax.ShapeDtypeStruct((n, GROUP_W), F32),
                   jax.ShapeDtypeStruct((n, GROUP_W), BF16),
                   jax.ShapeDtypeStruct((n, RWKV_PROJ_W), F32)],
        compiler_params=_params(("parallel",)),
        name="inproj",
    )(x, n1, w_in_b, qg, kg, gmean)


def _sb_step(z, times_v, u, carry, mask):
    sp = _softplus(z)
    log_keep = -sp
    if mask is not None:
        log_keep = jnp.where(mask, log_keep, 0.0)
    suffix = _dot(log_keep.astype(BF16), u)
    weight = jnp.exp((z - sp) + suffix + carry)
    if mask is not None:
        weight = jnp.where(mask, weight, 0.0)
    return times_v(weight.astype(BF16)), carry + jnp.sum(log_keep, axis=1, keepdims=True)


def _suffix_matrix(tk):
    j = lax.broadcasted_iota(jnp.int32, (tk, tk), 0)
    s = lax.broadcasted_iota(jnp.int32, (tk, tk), 1)
    return (j > s).astype(BF16)


def _sb_prompt_body(bias_ref, q_ref, k_ref, v_ref, u_ref, o_ref, *, tq):
    hp = pl.program_id(1)
    qi = pl.program_id(2)
    u = u_ref[...]
    q = q_ref[0]
    lane_head = lax.broadcasted_iota(jnp.int32, (tq, LANES), 1) // HEAD_DIM
    zero = jnp.zeros_like(q)
    qs = jnp.concatenate([jnp.where(lane_head == 0, q, zero),
                          jnp.where(lane_head == 1, q, zero)], axis=0)
    bias0 = bias_ref[hp * 2]
    bias1 = bias_ref[hp * 2 + 1]
    qpos = lax.broadcasted_iota(jnp.int32, (tq, tq), 0)
    kpos = lax.broadcasted_iota(jnp.int32, (tq, tq), 1)
    causal = jnp.concatenate([kpos < qpos] * 2, axis=0)

    def block(kb, carry, acc, mask):
        start = pl.multiple_of(kb * tq, tq)
        kh = k_ref[0, pl.ds(start, tq), :]
        vh = v_ref[0, pl.ds(start, tq), :]
        zz = _dot_nt(qs, kh)
        z = jnp.concatenate([zz[:tq] + bias0, zz[tq:] + bias1], axis=0)
        out, carry = _sb_step(z, lambda w: _dot(w, vh), u, carry, mask)
        return carry, acc + out

    carry, acc = block(qi, jnp.zeros((2 * tq, 1), F32), jnp.zeros((2 * tq, LANES), F32), causal)
    carry, acc = lax.fori_loop(
        0, qi, lambda i, c: block(qi - 1 - i, c[0], c[1], None), (carry, acc))
    o_ref[0] = jnp.where(lane_head == 0, acc[:tq], acc[tq:])


def _sb_prompt(q_b, k_b, v_b, bias, tq):
    b, s, _ = q_b.shape
    hpairs = GROUP_W // LANES
    return pl.pallas_call(
        functools.partial(_sb_prompt_body, tq=tq),
        grid=(b, hpairs, s // tq),
        in_specs=[pl.BlockSpec(memory_space=pltpu.SMEM),
                  pl.BlockSpec((1, tq, LANES), lambda bi, hp, qi: (bi, qi, hp)),
                  pl.BlockSpec((1, s, LANES), lambda bi, hp, qi: (bi, 0, hp)),
                  pl.BlockSpec((1, s, LANES), lambda bi, hp, qi: (bi, 0, hp)),
                  _const_spec((tq, tq))],
        out_specs=pl.BlockSpec((1, tq, LANES), lambda bi, hp, qi: (bi, qi, hp)),
        out_shape=jax.ShapeDtypeStruct((b, s, GROUP_W), F32),
        compiler_params=_params(("parallel", "parallel", "arbitrary")),
        name="sb_prompt",
    )(bias, q_b, k_b, v_b, _suffix_matrix(tq))


def _sb_decode_body(pt_ref, qbd_ref, bias_ref, knew_ref, vnew_ref, u_ref, *rest, pps, t_new):
    k_refs = rest[:pps]
    v_refs = rest[pps:2 * pps]
    o_ref, acc_ref, carry_ref = rest[2 * pps:]
    jj = pl.program_id(1)
    qbd = qbd_ref[0]
    bias = bias_ref[...]
    u = u_ref[...]
    m, page = bias.shape

    @pl.when(jj == 0)
    def _():
        pad = jnp.zeros((page - knew_ref.shape[1], GROUP_W), F32)
        kb = jnp.concatenate([knew_ref[0], pad], axis=0).astype(BF16)
        vb = jnp.concatenate([vnew_ref[0], pad], axis=0).astype(BF16)
        qpos = lax.broadcasted_iota(jnp.int32, (m, page), 0) // N_HEADS
        kpos = lax.broadcasted_iota(jnp.int32, (m, page), 1)
        z = _dot_nt(qbd, kb) + bias
        out, carry = _sb_step(z, lambda w: _dot(w, vb), u, jnp.zeros((m, page), F32), kpos < qpos)
        acc_ref[...] = out
        carry_ref[...] = carry

    z = jnp.concatenate([_dot(qbd, k_refs[i][0].astype(BF16)) + bias for i in range(pps)], axis=0)
    sp = _softplus(z)
    log_keep = -sp
    suffix = _dot(log_keep.astype(BF16), u)
    keep_sum = jnp.sum(log_keep, axis=1, keepdims=True)
    carry = carry_ref[...]
    carries = []
    for i in range(pps):
        carries.append(carry)
        carry = carry + keep_sum[i * m:(i + 1) * m]
    carry_ref[...] = carry
    weight = jnp.exp((z - sp) + suffix + jnp.concatenate(carries, axis=0)).astype(BF16)
    acc = acc_ref[...]
    for i in range(pps):
        acc = acc + _dot_nt(weight[i * m:(i + 1) * m], v_refs[i][0].astype(BF16))
    acc_ref[...] = acc

    @pl.when(jj == pl.num_programs(1) - 1)
    def _():
        row_head = lax.broadcasted_iota(jnp.int32, (m, GROUP_W), 0) % N_HEADS
        lane_head = lax.broadcasted_iota(jnp.int32, (m, GROUP_W), 1) // HEAD_DIM
        own = jnp.where(row_head == lane_head, acc, 0.0)
        o_ref[0] = jnp.sum(own.reshape(t_new, N_HEADS, GROUP_W), axis=1)


def _sb_decode(q_b, k_new, v_new, bias, cache_k, cache_v, page_table, pps):
    nb, t_new, _ = q_b.shape
    n_pages = page_table.shape[1]
    page = cache_k.shape[2]
    m = t_new * N_HEADS
    t_pad = -(-t_new // 8) * 8
    lane_head = jnp.arange(GROUP_W) // HEAD_DIM
    sel = (lane_head[None, :] == jnp.arange(N_HEADS)[:, None])
    qbd = jnp.where(sel[None, None], q_b[:, :, None, :], jnp.zeros((), BF16)).reshape(nb, m, GROUP_W)
    bias_tile = jnp.broadcast_to(jnp.tile(bias, t_new)[:, None], (m, page)).astype(F32)
    padn = ((0, 0), (0, t_pad - t_new), (0, 0))
    k_new = jnp.pad(k_new, padn)
    v_new = jnp.pad(v_new, padn)

    def page_spec(i):
        return pl.BlockSpec(
            (1, GROUP_W, page),
            lambda b, jj, pt, i=i: (pt[b * n_pages + (n_pages - 1 - (jj * pps + i))], 0, 0))

    seq = lambda r, w: pl.BlockSpec((1, r, w), lambda b, jj, pt: (b, 0, 0))
    grid_spec = pltpu.PrefetchScalarGridSpec(
        num_scalar_prefetch=1,
        grid=(nb, n_pages // pps),
        in_specs=[seq(m, GROUP_W),
                  pl.BlockSpec((m, page), lambda b, jj, pt: (0, 0)),
                  seq(t_pad, GROUP_W), seq(t_pad, GROUP_W),
                  pl.BlockSpec((page, page), lambda b, jj, pt: (0, 0))]
                 + [page_spec(i) for i in range(pps)] * 2,
        out_specs=seq(t_new, GROUP_W),
        scratch_shapes=[pltpu.VMEM((m, GROUP_W), F32), pltpu.VMEM((m, page), F32)],
    )
    return pl.pallas_call(
        functools.partial(_sb_decode_body, pps=pps, t_new=t_new),
        grid_spec=grid_spec,
        out_shape=jax.ShapeDtypeStruct((nb, t_new, GROUP_W), F32),
        compiler_params=_params(("parallel", "arbitrary")),
        name="sb_decode",
    )(page_table.reshape(-1), qbd, bias_tile, k_new, v_new, _suffix_matrix(page),
      *([cache_k] * pps), *([cache_v] * pps))


def _rwkv_prep_body(pr_ref, sh_ref, mu_ref, w0_ref, a0_ref, w2_ref, a2_ref, g2_ref,
                    kk_ref, ka_ref, rk_ref, gs_ref,
                    w_o, nkk_o, b_o, km_o, r_o, v_o, g_o, bonus_o, prev_ref, *, seq_len):
    pr = pr_ref[0]
    tc = pr.shape[0]
    row = lax.broadcasted_iota(jnp.int32, pr.shape, 0)
    rolled = pltpu.roll(pr, 1, axis=0)
    if seq_len is None:
        @pl.when(pl.program_id(1) == 0)
        def _():
            prev_ref[...] = sh_ref[0]

        prev = jnp.where(row == 0, prev_ref[...], rolled)
        prev_ref[...] = pr[tc - 1:tc, :]
    else:
        prev = jnp.where(row % seq_len == 0, sh_ref[0], rolled)
    xs = pr + (prev - pr) * mu_ref[...]
    r = xs[:, 0:GROUP_W]
    k = xs[:, GROUP_W:2 * GROUP_W]
    v = xs[:, 2 * GROUP_W:3 * GROUP_W]
    wa = xs[:, 3 * GROUP_W:3 * GROUP_W + LANES]
    gd = xs[:, 3 * GROUP_W + LANES:]
    gs = gs_ref[...]
    w_raw = -_softplus(-(w0_ref[...] + _dot(jnp.tanh(wa).astype(BF16), w2_ref[...]))) - 0.5
    w_o[0] = jnp.exp(-jnp.exp(w_raw))
    a = _sigmoid(a0_ref[...] + _dot(wa.astype(BF16), a2_ref[...]))
    g_o[0] = _dot(_sigmoid(gd).astype(BF16), g2_ref[...])
    kk = k * kk_ref[...]
    ss = _dot((kk * kk).astype(BF16), gs)
    kk = kk / jnp.maximum(jnp.sqrt(ss), 1e-12)
    nkk_o[0] = -kk
    b_o[0] = kk * a
    km = k * (1.0 + (a - 1.0) * ka_ref[...])
    km_o[0] = km
    r_o[0] = r
    v_o[0] = v
    bonus_o[0] = _dot((r * km * rk_ref[...]).astype(BF16), gs) * v


def _rwkv_prep(pr, shift, seq_len, mu, w0, a0, w2p, a2p, g2b, k_k, k_a, r_k, gsum, tc):
    b, s, pw = pr.shape
    blk = pl.BlockSpec((1, tc, GROUP_W), lambda bi, c: (bi, c, 0))
    vec = _const_spec((1, GROUP_W))
    if seq_len is None:
        shift_spec = pl.BlockSpec((1, 1, pw), lambda bi, c: (bi, 0, 0))
    else:
        assert tc % seq_len == 0
        shift_spec = pl.BlockSpec((1, tc, pw), lambda bi, c: (bi, c, 0))
    return pl.pallas_call(
        functools.partial(_rwkv_prep_body, seq_len=seq_len),
        grid=(b, s // tc),
        in_specs=[pl.BlockSpec((1, tc, pw), lambda bi, c: (bi, c, 0)),
                  shift_spec,
                  _const_spec((1, pw)), vec, vec,
                  _const_spec((LANES, GROUP_W)), _const_spec((LANES, GROUP_W)),
                  _const_spec((LANES, GROUP_W)), vec, vec, vec,
                  _const_spec((GROUP_W, GROUP_W))],
        out_specs=[blk] * 8,
        out_shape=[jax.ShapeDtypeStruct((b, s, GROUP_W), F32)] * 8,
        scratch_shapes=[pltpu.VMEM((1, pw), F32)],
        compiler_params=_params(("parallel", "arbitrary")),
        name="rwkv_prep",
    )(pr, shift, mu, w0, a0, w2p, a2p, g2b, k_k, k_a, r_k, gsum)


def _rwkv_rec_body(w_ref, nkk_ref, b_ref, km_ref, r_ref, v_ref, s0_ref, o_ref, st_ref, s_ref,
                   *spread_ref, tc, nv):
    c = pl.program_id(1)

    @pl.when(c == 0)
    def _():
        s_ref[...] = s0_ref[0]

    keyed = (w_ref, nkk_ref, b_ref, km_ref, r_ref)
    if spread_ref:
        low = lax.broadcasted_iota(jnp.int32, (HEAD_DIM // 2, LANES), 1) < LANES // 2

        def spread(i, carry):
            for dt in range(SPREAD_UNROLL):
                t = i * SPREAD_UNROLL + dt
                for j, ref in enumerate(keyed):
                    x = ref[t]
                    other = pltpu.roll(x, LANES // 2, axis=1)
                    spread_ref[0][j, t] = jnp.concatenate(
                        [jnp.where(low, x, other), jnp.where(low, other, x)], axis=0)
            return carry

        lax.fori_loop(0, tc // SPREAD_UNROLL, spread, 0)
        keyed = tuple(spread_ref[0].at[j] for j in range(len(keyed)))

    def step(t, carry):
        w, nkk, b, km, r = (ref[t] for ref in keyed)
        for vi in range(nv):
            s = s_ref[vi]
            sa = jnp.sum(s * nkk, axis=0, keepdims=True)
            s = s * w + sa * b + v_ref[t, pl.ds(vi, 1), :] * km
            s_ref[vi] = s
            o_ref[t, pl.ds(vi, 1), :] = jnp.sum(s * r, axis=0, keepdims=True)
        return carry

    lax.fori_loop(0, tc, step, 0)

    @pl.when(c == pl.num_programs(1) - 1)
    def _():
        st_ref[0] = s_ref[...]


def _rwkv_rec(w, nkk, b, km, r, v, s0, tc):
    t, nk, l = w.shape
    nv = v.shape[1]
    kvec = pl.BlockSpec((tc, nk, LANES), lambda g, c: (c, 0, g))
    vvec = pl.BlockSpec((tc, nv, LANES), lambda g, c: (c, 0, g))
    st = pl.BlockSpec((1, nv, HEAD_DIM, LANES), lambda g, c: (g, 0, 0, 0))
    scratch = [pltpu.VMEM((nv, HEAD_DIM, LANES), F32)]
    if nk < HEAD_DIM:
        assert tc % SPREAD_UNROLL == 0
        scratch.append(pltpu.VMEM((5, tc, HEAD_DIM, LANES), F32))
    return pl.pallas_call(
        functools.partial(_rwkv_rec_body, tc=tc, nv=nv),
        grid=(l // LANES, t // tc),
        in_specs=[kvec] * 5 + [vvec, st],
        out_specs=[vvec, st],
        out_shape=[jax.ShapeDtypeStruct((t, nv, l), F32),
                   jax.ShapeDtypeStruct(s0.shape, F32)],
        scratch_shapes=scratch,
        compiler_params=_params(("parallel", "arbitrary")),
        name="rwkv_rec",
    )(w, nkk, b, km, r, v, s0)


def _out_ffn_body(x_ref, att_ref, o_ref, g_ref, bonus_ref, gnw_ref, gnb_ref, gm_ref,
                  wo_ref, n2_ref, wg_ref, wu_ref, wd_ref, y_ref):
    gm = gm_ref[...]
    o = o_ref[...]
    mean = _dot(o.astype(BF16), gm)
    d = o - mean
    var = _dot((d * d).astype(BF16), gm)
    rw = (d * lax.rsqrt(var + GN_EPS) * gnw_ref[...] + gnb_ref[...] + bonus_ref[...]) * g_ref[...]
    h = (x_ref[...] + _dot(att_ref[...].astype(BF16), wo_ref[0:GROUP_W, :])
         + _dot(rw.astype(BF16), wo_ref[GROUP_W:, :]))
    hn = h * lax.rsqrt(jnp.mean(h * h, axis=-1, keepdims=True) + NORM_EPS) * n2_ref[...]
    hb = hn.astype(BF16)
    gate = _dot(hb, wg_ref[...])
    act = gate * _sigmoid(gate) * _dot(hb, wu_ref[...])
    y_ref[...] = h + _dot(act.astype(BF16), wd_ref[...])


def _out_ffn(x, att, o, g, bonus, gn_w, gn_b, gmean, wo_b, n2, wg_b, wu_b, wd_b, tm):
    n, d = x.shape
    dff = wg_b.shape[1]
    tok = lambda w: pl.BlockSpec((tm, w), lambda i: (i, 0))
    once = lambda shape: pl.BlockSpec(shape, lambda i: (0, 0), pipeline_mode=pl.Buffered(1))
    return pl.pallas_call(
        _out_ffn_body,
        grid=(n // tm,),
        in_specs=[tok(d), tok(GROUP_W), tok(GROUP_W), tok(GROUP_W), tok(GROUP_W),
                  once((1, GROUP_W)), once((1, GROUP_W)), once((GROUP_W, GROUP_W)),
                  once((2 * GROUP_W, d)), once((1, d)),
                  once((d, dff)), once((d, dff)), once((dff, d))],
        out_specs=tok(d),
        out_shape=jax.ShapeDtypeStruct((n, d), F32),
        compiler_params=_params(("parallel",)),
        name="out_ffn",
    )(x, att, o, g, bonus, gn_w, gn_b, gmean, wo_b, n2, wg_b, wu_b, wd_b)


def _to_chain_lanes(x, halves):
    b, t, _ = x.shape
    if halves:
        y = x.reshape(b, t, N_HEADS, 2, HEAD_DIM // 2).transpose(1, 4, 3, 0, 2)
        return y.reshape(t, HEAD_DIM // 2, 2 * b * N_HEADS)
    y = x.reshape(b, t, N_HEADS, HEAD_DIM).transpose(1, 3, 2, 0)
    return y.reshape(t, HEAD_DIM, N_HEADS * b)


def _from_chain_lanes(o, b, halves):
    t = o.shape[0]
    if halves:
        y = o.reshape(t, HEAD_DIM // 2, 2, b, N_HEADS).transpose(3, 0, 4, 2, 1)
    else:
        y = o.reshape(t, HEAD_DIM, N_HEADS, b).transpose(3, 0, 2, 1)
    return y.reshape(b, t, GROUP_W)


def _state_to_chain_lanes(s):
    b = s.shape[0]
    y = s.transpose(1, 2, 3, 0)
    if b % LANES == 0:
        y = y.reshape(N_HEADS, HEAD_DIM, HEAD_DIM, b // LANES, LANES).transpose(0, 3, 1, 2, 4)
    else:
        y = y.transpose(1, 2, 0, 3)
    return y.reshape(N_HEADS * b // LANES, HEAD_DIM, HEAD_DIM, LANES)


def _state_from_chain_lanes(s, b, halves):
    if halves:
        y = s.reshape(HEAD_DIM // 2, HEAD_DIM, 2, b, N_HEADS).transpose(3, 4, 2, 0, 1)
    elif b % LANES == 0:
        y = s.reshape(N_HEADS, b // LANES, HEAD_DIM, HEAD_DIM, LANES).transpose(1, 4, 0, 2, 3)
    else:
        y = s.reshape(HEAD_DIM, HEAD_DIM, N_HEADS, b).transpose(3, 2, 0, 1)
    return y.reshape(b, N_HEADS, HEAD_DIM, HEAD_DIM)


def _rwkv_group(pr, shift_prev, s0, prm, tc_prep, tc_rec):
    b, t, pw = pr.shape
    weights = (prm["mu"], prm["w0"], prm["a0"], prm["w2p"], prm["a2p"], prm["g2b"],
               prm["k_k"], prm["k_a"], prm["r_k"], prm["gsum"])
    if t >= tc_prep:
        vecs = _rwkv_prep(pr, shift_prev[:, None, :], None, *weights, tc_prep)
    else:
        starts = jnp.zeros((b, t, pw), F32).at[:, 0, :].set(shift_prev)
        vecs = _rwkv_prep(pr.reshape(1, b * t, pw), starts.reshape(1, b * t, pw), t,
                          *weights, min(tc_prep, b * t))
        vecs = [a.reshape(b, t, GROUP_W) for a in vecs]
    w, nkk, bv, km, r, v, g, bonus = vecs
    halves = 2 * b * N_HEADS == LANES
    vecs = [_to_chain_lanes(a, halves) for a in (w, nkk, bv, km, r, v)]
    if halves:
        assert s0 is None
        s0c = jnp.zeros((1, HEAD_DIM // 2, HEAD_DIM, LANES), F32)
    else:
        s0c = _state_to_chain_lanes(s0)
    o, s_last = _rwkv_rec(*vecs, s0c, tc_rec)
    return _from_chain_lanes(o, b, halves), g, bonus, _state_from_chain_lanes(s_last, b, halves)


def _group_matrix(scale):
    i = jnp.arange(GROUP_W) // HEAD_DIM
    return jnp.where(i[:, None] == i[None, :], scale, 0.0).astype(BF16)


def kernel(x_prompt, x_sample, cache_k, cache_v, state_rwkv, state_shift, page_table, w_in, q_norm, k_norm, sb_bias, mu_shift, w0, w2, a0, a2, g2, k_k, k_a, r_k, gn_w, gn_b, w_o, norm1, norm2, w_gate, w_up, w_down):
    depth = w_in.shape[0]
    bp, sp, d = x_prompt.shape
    bs, ts, _ = x_sample.shape
    gmean = _group_matrix(1.0 / HEAD_DIM)
    gsum = _group_matrix(1.0)
    yp = x_prompt.reshape(bp * sp, d)
    ys = x_sample.reshape(bs * ts, d)
    outs = [[] for _ in range(8)]
    for l in range(depth):
        row = lambda a: a[l][None, :]
        w_in_b = w_in[l].astype(BF16)
        qg = jnp.tile(q_norm[l], N_HEADS)[None, :]
        kg = jnp.tile(k_norm[l], N_HEADS)[None, :]
        lora_pad = jnp.zeros((LANES // 2, GROUP_W), F32)
        prm = dict(mu=row(mu_shift), w0=row(w0), a0=row(a0),
                   w2p=jnp.concatenate([w2[l], lora_pad], axis=0).astype(BF16),
                   a2p=jnp.concatenate([lora_pad, a2[l]], axis=0).astype(BF16),
                   g2b=g2[l].astype(BF16), k_k=row(k_k), k_a=row(k_a), r_k=row(r_k), gsum=gsum)
        ffn = (row(gn_w), row(gn_b), gmean, w_o[l].astype(BF16), row(norm2),
               w_gate[l].astype(BF16), w_up[l].astype(BF16), w_down[l].astype(BF16))

        tm = min(TOKEN_TILE, bp * sp)
        q_b, k_p, k_b, v_p, v_b, pr = _inproj(yp, row(norm1), w_in_b, qg, kg, gmean, tm)
        seq = lambda a: a.reshape(bp, sp, a.shape[-1])
        att = _sb_prompt(seq(q_b), seq(k_b), seq(v_b), sb_bias[l], min(ATT_TILE, sp))
        pr = seq(pr)
        o, g, bonus, s_p = _rwkv_group(pr, jnp.zeros((bp, RWKV_PROJ_W), F32), None, prm,
                                       PREP_CHUNK, min(REC_CHUNK, sp))
        flat = lambda a: a.reshape(bp * sp, GROUP_W)
        yp = _out_ffn(yp, flat(att), flat(o), flat(g), flat(bonus), *ffn, tm)
        outs[0].append(k_p.reshape(bp, sp, N_HEADS, HEAD_DIM))
        outs[1].append(v_p.reshape(bp, sp, N_HEADS, HEAD_DIM))
        outs[2].append(s_p)
        outs[3].append(pr[:, -1])

        tm = min(TOKEN_TILE, bs * ts)
        q_b, k_s, _, v_s, _, pr = _inproj(ys, row(norm1), w_in_b, qg, kg, gmean, tm)
        seq = lambda a: a.reshape(bs, ts, a.shape[-1])
        n_pool, page = cache_k.shape[1], cache_k.shape[2]
        pool = lambda c: c[l].transpose(0, 2, 3, 1).reshape(n_pool, GROUP_W, page)
        att = _sb_decode(seq(q_b), seq(k_s), seq(v_s), sb_bias[l],
                         pool(cache_k), pool(cache_v), page_table,
                         min(PAGES_PER_STEP, page_table.shape[1]))
        pr = seq(pr)
        o, g, bonus, s_s = _rwkv_group(pr, state_shift[l], state_rwkv[l], prm,
                                       PREP_CHUNK, min(REC_CHUNK, ts))
        flat = lambda a: a.reshape(bs * ts, GROUP_W)
        ys = _out_ffn(ys, flat(att), flat(o), flat(g), flat(bonus), *ffn, tm)
        outs[4].append(k_s.reshape(bs, ts, N_HEADS, HEAD_DIM))
        outs[5].append(v_s.reshape(bs, ts, N_HEADS, HEAD_DIM))
        outs[6].append(s_s)
        outs[7].append(pr[:, -1])
    stacked = [jnp.stack(o) for o in outs]
    return (yp.reshape(bp, sp, d), ys.reshape(bs, ts, d), *stacked)
```

```python
import functools

import jax
import jax.numpy as jnp
from jax import lax
from jax.experimental import pallas as pl
from jax.experimental.pallas import tpu as pltpu

F32 = jnp.float32
BF16 = jnp.bfloat16

HEAD_DIM = 64
N_HEADS = 8
GROUP_W = N_HEADS * HEAD_DIM
LORA_W = 256
RWKV_PROJ_W = 3 * GROUP_W + LORA_W
SB_SCALE = HEAD_DIM ** -0.5
LOG2E = 1.4426950408889634
NORM_EPS = 1e-6
GN_EPS = 64e-5
LANES = 128
VMEM_LIMIT = 56 * 1024 * 1024
TOKEN_TILE = 256
ATT_TILE = 256
PREP_CHUNK = 256
REC_CHUNK = 32
SPREAD_UNROLL = 8
PAGES_PER_STEP = 8
PAGE_SLOTS = 3


def _dot(a, b):
    return jnp.dot(a, b, preferred_element_type=F32)


def _dot_nt(a, b):
    return lax.dot_general(a, b, (((1,), (1,)), ((), ())), preferred_element_type=F32)


def _softplus(y):
    return jnp.maximum(y, 0.0) + jnp.log(1.0 + jnp.exp2(jnp.abs(y) * -LOG2E))


def _sigmoid(y):
    return 1.0 / (1.0 + jnp.exp(-y))


def _params(sem):
    return pltpu.CompilerParams(dimension_semantics=sem, vmem_limit_bytes=VMEM_LIMIT)


def _const_spec(shape):
    nd = len(shape)
    return pl.BlockSpec(shape, lambda *_: (0,) * nd)


def _inproj_body(x_ref, n1_ref, w_ref, qg_ref, kg_ref, gm_ref,
                 q_ref, k_ref, kb_ref, v_ref, vb_ref, pr_ref):
    x = x_ref[...]
    xn = x * lax.rsqrt(jnp.mean(x * x, axis=-1, keepdims=True) + NORM_EPS) * n1_ref[...]
    xb = xn.astype(BF16)

    def head_norm(t, g):
        ms = _dot((t * t).astype(BF16), gm_ref[...])
        return t * lax.rsqrt(ms + NORM_EPS) * g

    q = _dot(xb, w_ref[:, 0:GROUP_W])
    q_ref[...] = (head_norm(q, qg_ref[...]) * SB_SCALE).astype(BF16)
    k = head_norm(_dot(xb, w_ref[:, GROUP_W:2 * GROUP_W]), kg_ref[...])
    kb_ref[...] = k.astype(BF16)
    v = _dot(xb, w_ref[:, 2 * GROUP_W:3 * GROUP_W])
    vb_ref[...] = v.astype(BF16)
    for h in range(N_HEADS):
        k_ref[:, h, :] = k[:, h * HEAD_DIM:(h + 1) * HEAD_DIM]
        v_ref[:, h, :] = v[:, h * HEAD_DIM:(h + 1) * HEAD_DIM]
    pr_ref[...] = _dot(xb, w_ref[:, 3 * GROUP_W:])


def _inproj(x, n1, w_in_b, qg, kg, gmean, tm):
    n, d = x.shape
    pw = w_in_b.shape[1]
    tok = lambda w: pl.BlockSpec((tm, w), lambda i: (i, 0))
    heads = pl.BlockSpec((tm, N_HEADS, HEAD_DIM), lambda i: (i, 0, 0))
    return pl.pallas_call(
        _inproj_body,
        grid=(n // tm,),
        in_specs=[tok(d), _const_spec((1, d)), _const_spec((d, pw)),
                  _const_spec((1, GROUP_W)), _const_spec((1, GROUP_W)),
                  _const_spec((GROUP_W, GROUP_W))],
        out_specs=[tok(GROUP_W), heads, tok(GROUP_W), heads, tok(GROUP_W),
                   tok(RWKV_PROJ_W)],
        out_shape=[jax.ShapeDtypeStruct((n, GROUP_W), BF16),
                   jax.ShapeDtypeStruct((n, N_HEADS, HEAD_DIM), F32),
                   jax.ShapeDtypeStruct((n, GROUP_W), BF16),
                   jax.ShapeDtypeStruct((n, N_HEADS, HEAD_DIM), F32),
                   jax.ShapeDtypeStruct((n, GROUP_W), BF16),
                   jax.ShapeDtypeStruct((n, RWKV_PROJ_W), F32)],
        compiler_params=_params(("parallel",)),
        name="inproj",
    )(x, n1, w_in_b, qg, kg, gmean)


def _sb_weights(z, neg_u, carry, mask):
    sp = _softplus(z)
    if mask is not None:
        sp = jnp.where(mask, sp, 0.0)
    suffix = _dot(sp.astype(BF16), neg_u)
    weight = jnp.exp2(((z - sp) + suffix + carry) * LOG2E)
    if mask is not None:
        weight = jnp.where(mask, weight, 0.0)
    return weight.astype(BF16), carry - jnp.sum(sp, axis=1, keepdims=True)


def _neg_suffix_matrix(tk):
    j = lax.broadcasted_iota(jnp.int32, (tk, tk), 0)
    s = lax.broadcasted_iota(jnp.int32, (tk, tk), 1)
    return jnp.where(j > s, -1.0, 0.0).astype(BF16)


def _sb_prompt_body(bias_ref, q_ref, k_ref, v_ref, u_ref, o_ref, *, tq):
    hp = pl.program_id(1)
    qi = pl.program_id(2)
    u = u_ref[...]
    q = q_ref[0]
    lane_head = lax.broadcasted_iota(jnp.int32, (tq, LANES), 1) // HEAD_DIM
    zero = jnp.zeros_like(q)
    qs = jnp.concatenate([jnp.where(lane_head == 0, q, zero),
                          jnp.where(lane_head == 1, q, zero)], axis=0)
    bias0 = bias_ref[hp * 2]
    bias1 = bias_ref[hp * 2 + 1]
    qpos = lax.broadcasted_iota(jnp.int32, (tq, tq), 0)
    kpos = lax.broadcasted_iota(jnp.int32, (tq, tq), 1)
    causal = jnp.concatenate([kpos < qpos] * 2, axis=0)

    def logits(kb):
        start = pl.multiple_of(kb * tq, tq)
        zz = _dot_nt(qs, k_ref[0, pl.ds(start, tq), :])
        return jnp.concatenate([zz[:tq] + bias0, zz[tq:] + bias1], axis=0)

    def values(kb, w):
        start = pl.multiple_of(kb * tq, tq)
        return _dot(w, v_ref[0, pl.ds(start, tq), :])

    w, carry = _sb_weights(logits(qi), u, jnp.zeros((2 * tq, 1), F32), causal)

    def body(i, state):
        z, w_newer, carry, acc = state
        kb = qi - 1 - i
        z_older = logits(jnp.maximum(kb - 1, 0))
        out_newer = values(kb + 1, w_newer)
        w, carry = _sb_weights(z, u, carry, None)
        return z_older, w, carry, acc + out_newer

    state = (logits(jnp.maximum(qi - 1, 0)), w, carry, jnp.zeros((2 * tq, LANES), F32))
    _, w, _, acc = lax.fori_loop(0, qi, body, state)
    acc = acc + values(0, w)
    o_ref[0] = jnp.where(lane_head == 0, acc[:tq], acc[tq:])


def _sb_prompt(q_b, k_b, v_b, bias, tq):
    b, s, _ = q_b.shape
    hpairs = GROUP_W // LANES
    return pl.pallas_call(
        functools.partial(_sb_prompt_body, tq=tq),
        grid=(b, hpairs, s // tq),
        in_specs=[pl.BlockSpec(memory_space=pltpu.SMEM),
                  pl.BlockSpec((1, tq, LANES), lambda bi, hp, qi: (bi, qi, hp)),
                  pl.BlockSpec((1, s, LANES), lambda bi, hp, qi: (bi, 0, hp)),
                  pl.BlockSpec((1, s, LANES), lambda bi, hp, qi: (bi, 0, hp)),
                  _const_spec((tq, tq))],
        out_specs=pl.BlockSpec((1, tq, LANES), lambda bi, hp, qi: (bi, qi, hp)),
        out_shape=jax.ShapeDtypeStruct((b, s, GROUP_W), F32),
        compiler_params=_params(("parallel", "parallel", "arbitrary")),
        name="sb_prompt",
    )(bias, q_b, k_b, v_b, _neg_suffix_matrix(tq))


def _sb_decode_body(pt_ref, qbd_ref, bias_ref, knew_ref, vnew_ref, u_ref, ck_hbm, cv_hbm, o_ref,
                    acc_ref, carry_ref, kbuf, vbuf, sem, *, pps, t_new, n_pages):
    jj = pl.program_id(1)
    steps_per_seq = pl.num_programs(1)
    step = pl.program_id(0) * steps_per_seq + jj
    n_steps = pl.num_programs(0) * steps_per_seq
    qbd = qbd_ref[0]
    bias = bias_ref[...]
    u = u_ref[...]
    m, page = bias.shape

    def page_copies(s):
        slot = s % PAGE_SLOTS
        first = (s // steps_per_seq) * n_pages + (n_pages - 1) - (s % steps_per_seq) * pps
        copies = []
        for i in range(pps):
            pid = pt_ref[first - i]
            copies.append(pltpu.make_async_copy(ck_hbm.at[pid], kbuf.at[slot, i], sem.at[slot, 0, i]))
            copies.append(pltpu.make_async_copy(cv_hbm.at[pid], vbuf.at[slot, i], sem.at[slot, 1, i]))
        return copies

    @pl.when(step == 0)
    def _():
        for s in range(PAGE_SLOTS - 1):
            @pl.when(s < n_steps)
            def _(s=s):
                for c in page_copies(s):
                    c.start()

    @pl.when(step + (PAGE_SLOTS - 1) < n_steps)
    def _():
        for c in page_copies(step + (PAGE_SLOTS - 1)):
            c.start()

    for c in page_copies(step):
        c.wait()
    slot = step % PAGE_SLOTS
    k_refs = [kbuf.at[slot, i] for i in range(pps)]
    v_refs = [vbuf.at[slot, i] for i in range(pps)]

    @pl.when(jj == 0)
    def _():
        pad = jnp.zeros((page - knew_ref.shape[1], GROUP_W), F32)
        kb = jnp.concatenate([knew_ref[0], pad], axis=0).astype(BF16)
        vb = jnp.concatenate([vnew_ref[0], pad], axis=0).astype(BF16)
        qpos = lax.broadcasted_iota(jnp.int32, (m, page), 0) // N_HEADS
        kpos = lax.broadcasted_iota(jnp.int32, (m, page), 1)
        w, carry = _sb_weights(_dot_nt(qbd, kb) + bias, u, jnp.zeros((m, page), F32), kpos < qpos)
        acc_ref[...] = _dot(w, vb)
        carry_ref[...] = carry

    z = jnp.concatenate([_dot(qbd, k_refs[i][...].astype(BF16)) + bias for i in range(pps)], axis=0)
    sp = _softplus(z)
    suffix = _dot(sp.astype(BF16), u)
    sp_sum = jnp.sum(sp, axis=1, keepdims=True)
    carry = carry_ref[...]
    carries = []
    for i in range(pps):
        carries.append(carry)
        carry = carry - sp_sum[i * m:(i + 1) * m]
    carry_ref[...] = carry
    weight = jnp.exp2(((z - sp) + suffix + jnp.concatenate(carries, axis=0)) * LOG2E).astype(BF16)
    acc = acc_ref[...]
    for i in range(pps):
        acc = acc + _dot_nt(weight[i * m:(i + 1) * m], v_refs[i][...].astype(BF16))
    acc_ref[...] = acc

    @pl.when(jj == pl.num_programs(1) - 1)
    def _():
        row_head = lax.broadcasted_iota(jnp.int32, (m, GROUP_W), 0) % N_HEADS
        lane_head = lax.broadcasted_iota(jnp.int32, (m, GROUP_W), 1) // HEAD_DIM
        own = jnp.where(row_head == lane_head, acc, 0.0)
        o_ref[0] = jnp.sum(own.reshape(t_new, N_HEADS, GROUP_W), axis=1)


def _sb_decode(q_b, k_new, v_new, bias, cache_k, cache_v, page_table, pps):
    nb, t_new, _ = q_b.shape
    n_pages = page_table.shape[1]
    page = cache_k.shape[2]
    m = t_new * N_HEADS
    t_pad = -(-t_new // 8) * 8
    lane_head = jnp.arange(GROUP_W) // HEAD_DIM
    sel = (lane_head[None, :] == jnp.arange(N_HEADS)[:, None])
    qbd = jnp.where(sel[None, None], q_b[:, :, None, :], jnp.zeros((), BF16)).reshape(nb, m, GROUP_W)
    bias_tile = jnp.broadcast_to(jnp.tile(bias, t_new)[:, None], (m, page)).astype(F32)
    padn = ((0, 0), (0, t_pad - t_new), (0, 0))
    k_new = jnp.pad(k_new, padn)
    v_new = jnp.pad(v_new, padn)

    seq = lambda r, w: pl.BlockSpec((1, r, w), lambda b, jj, pt: (b, 0, 0))
    grid_spec = pltpu.PrefetchScalarGridSpec(
        num_scalar_prefetch=1,
        grid=(nb, n_pages // pps),
        in_specs=[seq(m, GROUP_W),
                  pl.BlockSpec((m, page), lambda b, jj, pt: (0, 0)),
                  seq(t_pad, GROUP_W), seq(t_pad, GROUP_W),
                  pl.BlockSpec((page, page), lambda b, jj, pt: (0, 0)),
                  pl.BlockSpec(memory_space=pl.ANY), pl.BlockSpec(memory_space=pl.ANY)],
        out_specs=seq(t_new, GROUP_W),
        scratch_shapes=[pltpu.VMEM((m, GROUP_W), F32), pltpu.VMEM((m, page), F32),
                        pltpu.VMEM((PAGE_SLOTS, pps, GROUP_W, page), F32),
                        pltpu.VMEM((PAGE_SLOTS, pps, GROUP_W, page), F32),
                        pltpu.SemaphoreType.DMA((PAGE_SLOTS, 2, pps))],
    )
    return pl.pallas_call(
        functools.partial(_sb_decode_body, pps=pps, t_new=t_new, n_pages=n_pages),
        grid_spec=grid_spec,
        out_shape=jax.ShapeDtypeStruct((nb, t_new, GROUP_W), F32),
        compiler_params=_params(("arbitrary", "arbitrary")),
        name="sb_decode",
    )(page_table.reshape(-1), qbd, bias_tile, k_new, v_new, _neg_suffix_matrix(page),
      cache_k, cache_v)


def _rwkv_prep_body(pr_ref, sh_ref, mu_ref, w0_ref, a0_ref, w2_ref, a2_ref, g2_ref,
                    kk_ref, ka_ref, rk_ref, gs_ref,
                    w_o, nkk_o, b_o, km_o, r_o, v_o, g_o, bonus_o, prev_ref, *, seq_len):
    pr = pr_ref[0]
    tc = pr.shape[0]
    row = lax.broadcasted_iota(jnp.int32, pr.shape, 0)
    rolled = pltpu.roll(pr, 1, axis=0)
    if seq_len is None:
        @pl.when(pl.program_id(1) == 0)
        def _():
            prev_ref[...] = sh_ref[0]

        prev = jnp.where(row == 0, prev_ref[...], rolled)
        prev_ref[...] = pr[tc - 1:tc, :]
    else:
        prev = jnp.where(row % seq_len == 0, sh_ref[0], rolled)
    xs = pr + (prev - pr) * mu_ref[...]
    r = xs[:, 0:GROUP_W]
    k = xs[:, GROUP_W:2 * GROUP_W]
    v = xs[:, 2 * GROUP_W:3 * GROUP_W]
    wa = xs[:, 3 * GROUP_W:3 * GROUP_W + LANES]
    gd = xs[:, 3 * GROUP_W + LANES:]
    gs = gs_ref[...]
    w_raw = -_softplus(-(w0_ref[...] + _dot(jnp.tanh(wa).astype(BF16), w2_ref[...]))) - 0.5
    w_o[0] = jnp.exp(-jnp.exp(w_raw))
    a = _sigmoid(a0_ref[...] + _dot(wa.astype(BF16), a2_ref[...]))
    g_o[0] = _dot(_sigmoid(gd).astype(BF16), g2_ref[...])
    kk = k * kk_ref[...]
    ss = _dot((kk * kk).astype(BF16), gs)
    kk = kk / jnp.maximum(jnp.sqrt(ss), 1e-12)
    nkk_o[0] = -kk
    b_o[0] = kk * a
    km = k * (1.0 + (a - 1.0) * ka_ref[...])
    km_o[0] = km
    r_o[0] = r
    v_o[0] = v
    bonus_o[0] = _dot((r * km * rk_ref[...]).astype(BF16), gs) * v


def _rwkv_prep(pr, shift, seq_len, mu, w0, a0, w2p, a2p, g2b, k_k, k_a, r_k, gsum, tc):
    b, s, pw = pr.shape
    blk = pl.BlockSpec((1, tc, GROUP_W), lambda bi, c: (bi, c, 0))
    vec = _const_spec((1, GROUP_W))
    if seq_len is None:
        shift_spec = pl.BlockSpec((1, 1, pw), lambda bi, c: (bi, 0, 0))
    else:
        assert tc % seq_len == 0
        shift_spec = pl.BlockSpec((1, tc, pw), lambda bi, c: (bi, c, 0))
    return pl.pallas_call(
        functools.partial(_rwkv_prep_body, seq_len=seq_len),
        grid=(b, s // tc),
        in_specs=[pl.BlockSpec((1, tc, pw), lambda bi, c: (bi, c, 0)),
                  shift_spec,
                  _const_spec((1, pw)), vec, vec,
                  _const_spec((LANES, GROUP_W)), _const_spec((LANES, GROUP_W)),
                  _const_spec((LANES, GROUP_W)), vec, vec, vec,
                  _const_spec((GROUP_W, GROUP_W))],
        out_specs=[blk] * 8,
        out_shape=[jax.ShapeDtypeStruct((b, s, GROUP_W), F32)] * 8,
        scratch_shapes=[pltpu.VMEM((1, pw), F32)],
        compiler_params=_params(("parallel", "arbitrary")),
        name="rwkv_prep",
    )(pr, shift, mu, w0, a0, w2p, a2p, g2b, k_k, k_a, r_k, gsum)


def _rwkv_rec_body(w_ref, nkk_ref, b_ref, km_ref, r_ref, v_ref, s0_ref, o_ref, st_ref, s_ref,
                   *spread_ref, tc, nv):
    c = pl.program_id(1)

    @pl.when(c == 0)
    def _():
        s_ref[...] = s0_ref[0]

    keyed = (w_ref, nkk_ref, b_ref, km_ref, r_ref)
    if spread_ref:
        low = lax.broadcasted_iota(jnp.int32, (HEAD_DIM // 2, LANES), 1) < LANES // 2

        def spread(i, carry):
            for dt in range(SPREAD_UNROLL):
                t = i * SPREAD_UNROLL + dt
                for j, ref in enumerate(keyed):
                    x = ref[t]
                    other = pltpu.roll(x, LANES // 2, axis=1)
                    spread_ref[0][j, t] = jnp.concatenate(
                        [jnp.where(low, x, other), jnp.where(low, other, x)], axis=0)
            return carry

        lax.fori_loop(0, tc // SPREAD_UNROLL, spread, 0)
        keyed = tuple(spread_ref[0].at[j] for j in range(len(keyed)))

    def step(t, carry):
        w, nkk, b, km, r = (ref[t] for ref in keyed)
        for vi in range(nv):
            s = s_ref[vi]
            sa = jnp.sum(s * nkk, axis=0, keepdims=True)
            s = s * w + sa * b + v_ref[t, pl.ds(vi, 1), :] * km
            s_ref[vi] = s
            o_ref[t, pl.ds(vi, 1), :] = jnp.sum(s * r, axis=0, keepdims=True)
        return carry

    lax.fori_loop(0, tc, step, 0)

    @pl.when(c == pl.num_programs(1) - 1)
    def _():
        st_ref[0] = s_ref[...]


def _rwkv_rec(w, nkk, b, km, r, v, s0, tc):
    t, nk, l = w.shape
    nv = v.shape[1]
    kvec = pl.BlockSpec((tc, nk, LANES), lambda g, c: (c, 0, g))
    vvec = pl.BlockSpec((tc, nv, LANES), lambda g, c: (c, 0, g))
    st = pl.BlockSpec((1, nv, HEAD_DIM, LANES), lambda g, c: (g, 0, 0, 0))
    scratch = [pltpu.VMEM((nv, HEAD_DIM, LANES), F32)]
    if nk < HEAD_DIM:
        assert tc % SPREAD_UNROLL == 0
        scratch.append(pltpu.VMEM((5, tc, HEAD_DIM, LANES), F32))
    return pl.pallas_call(
        functools.partial(_rwkv_rec_body, tc=tc, nv=nv),
        grid=(l // LANES, t // tc),
        in_specs=[kvec] * 5 + [vvec, st],
        out_specs=[vvec, st],
        out_shape=[jax.ShapeDtypeStruct((t, nv, l), F32),
                   jax.ShapeDtypeStruct(s0.shape, F32)],
        scratch_shapes=scratch,
        compiler_params=_params(("parallel", "arbitrary")),
        name="rwkv_rec",
    )(w, nkk, b, km, r, v, s0)


def _out_ffn_body(x_ref, att_ref, o_ref, g_ref, bonus_ref, gnw_ref, gnb_ref, gm_ref,
                  wo_ref, n2_ref, wg_ref, wu_ref, wd_ref, y_ref):
    gm = gm_ref[...]
    o = o_ref[...]
    mean = _dot(o.astype(BF16), gm)
    d = o - mean
    var = _dot((d * d).astype(BF16), gm)
    rw = (d * lax.rsqrt(var + GN_EPS) * gnw_ref[...] + gnb_ref[...] + bonus_ref[...]) * g_ref[...]
    h = (x_ref[...] + _dot(att_ref[...].astype(BF16), wo_ref[0:GROUP_W, :])
         + _dot(rw.astype(BF16), wo_ref[GROUP_W:, :]))
    hn = h * lax.rsqrt(jnp.mean(h * h, axis=-1, keepdims=True) + NORM_EPS) * n2_ref[...]
    hb = hn.astype(BF16)
    gate = _dot(hb, wg_ref[...])
    act = gate * _sigmoid(gate) * _dot(hb, wu_ref[...])
    y_ref[...] = h + _dot(act.astype(BF16), wd_ref[...])


def _out_ffn(x, att, o, g, bonus, gn_w, gn_b, gmean, wo_b, n2, wg_b, wu_b, wd_b, tm):
    n, d = x.shape
    dff = wg_b.shape[1]
    tok = lambda w: pl.BlockSpec((tm, w), lambda i: (i, 0))
    once = lambda shape: pl.BlockSpec(shape, lambda i: (0, 0), pipeline_mode=pl.Buffered(1))
    return pl.pallas_call(
        _out_ffn_body,
        grid=(n // tm,),
        in_specs=[tok(d), tok(GROUP_W), tok(GROUP_W), tok(GROUP_W), tok(GROUP_W),
                  once((1, GROUP_W)), once((1, GROUP_W)), once((GROUP_W, GROUP_W)),
                  once((2 * GROUP_W, d)), once((1, d)),
                  once((d, dff)), once((d, dff)), once((dff, d))],
        out_specs=tok(d),
        out_shape=jax.ShapeDtypeStruct((n, d), F32),
        compiler_params=_params(("parallel",)),
        name="out_ffn",
    )(x, att, o, g, bonus, gn_w, gn_b, gmean, wo_b, n2, wg_b, wu_b, wd_b)


def _to_chain_lanes(x, halves):
    b, t, _ = x.shape
    if halves:
        y = x.reshape(b, t, N_HEADS, 2, HEAD_DIM // 2).transpose(1, 4, 3, 0, 2)
        return y.reshape(t, HEAD_DIM // 2, 2 * b * N_HEADS)
    y = x.reshape(b, t, N_HEADS, HEAD_DIM).transpose(1, 3, 2, 0)
    return y.reshape(t, HEAD_DIM, N_HEADS * b)


def _from_chain_lanes(o, b, halves):
    t = o.shape[0]
    if halves:
        y = o.reshape(t, HEAD_DIM // 2, 2, b, N_HEADS).transpose(3, 0, 4, 2, 1)
    else:
        y = o.reshape(t, HEAD_DIM, N_HEADS, b).transpose(3, 0, 2, 1)
    return y.reshape(b, t, GROUP_W)


def _state_to_chain_lanes(s):
    b = s.shape[0]
    y = s.transpose(1, 2, 3, 0)
    if b % LANES == 0:
        y = y.reshape(N_HEADS, HEAD_DIM, HEAD_DIM, b // LANES, LANES).transpose(0, 3, 1, 2, 4)
    else:
        y = y.transpose(1, 2, 0, 3)
    return y.reshape(N_HEADS * b // LANES, HEAD_DIM, HEAD_DIM, LANES)


def _state_from_chain_lanes(s, b, halves):
    if halves:
        y = s.reshape(HEAD_DIM // 2, HEAD_DIM, 2, b, N_HEADS).transpose(3, 4, 2, 0, 1)
    elif b % LANES == 0:
        y = s.reshape(N_HEADS, b // LANES, HEAD_DIM, HEAD_DIM, LANES).transpose(1, 4, 0, 2, 3)
    else:
        y = s.reshape(HEAD_DIM, HEAD_DIM, N_HEADS, b).transpose(3, 2, 0, 1)
    return y.reshape(b, N_HEADS, HEAD_DIM, HEAD_DIM)


def _rwkv_group(pr, shift_prev, s0, prm, tc_prep, tc_rec):
    b, t, pw = pr.shape
    weights = (prm["mu"], prm["w0"], prm["a0"], prm["w2p"], prm["a2p"], prm["g2b"],
               prm["k_k"], prm["k_a"], prm["r_k"], prm["gsum"])
    if t >= tc_prep:
        vecs = _rwkv_prep(pr, shift_prev[:, None, :], None, *weights, tc_prep)
    else:
        starts = jnp.zeros((b, t, pw), F32).at[:, 0, :].set(shift_prev)
        vecs = _rwkv_prep(pr.reshape(1, b * t, pw), starts.reshape(1, b * t, pw), t,
                          *weights, min(tc_prep, b * t))
        vecs = [a.reshape(b, t, GROUP_W) for a in vecs]
    w, nkk, bv, km, r, v, g, bonus = vecs
    halves = 2 * b * N_HEADS == LANES
    vecs = [_to_chain_lanes(a, halves) for a in (w, nkk, bv, km, r, v)]
    if halves:
        assert s0 is None
        s0c = jnp.zeros((1, HEAD_DIM // 2, HEAD_DIM, LANES), F32)
    else:
        s0c = _state_to_chain_lanes(s0)
    o, s_last = _rwkv_rec(*vecs, s0c, tc_rec)
    return _from_chain_lanes(o, b, halves), g, bonus, _state_from_chain_lanes(s_last, b, halves)


def _group_matrix(scale):
    i = jnp.arange(GROUP_W) // HEAD_DIM
    return jnp.where(i[:, None] == i[None, :], scale, 0.0).astype(BF16)


def kernel(x_prompt, x_sample, cache_k, cache_v, state_rwkv, state_shift, page_table, w_in, q_norm, k_norm, sb_bias, mu_shift, w0, w2, a0, a2, g2, k_k, k_a, r_k, gn_w, gn_b, w_o, norm1, norm2, w_gate, w_up, w_down):
    depth = w_in.shape[0]
    bp, sp, d = x_prompt.shape
    bs, ts, _ = x_sample.shape
    gmean = _group_matrix(1.0 / HEAD_DIM)
    gsum = _group_matrix(1.0)
    yp = x_prompt.reshape(bp * sp, d)
    ys = x_sample.reshape(bs * ts, d)
    outs = [[] for _ in range(8)]
    for l in range(depth):
        row = lambda a: a[l][None, :]
        w_in_b = w_in[l].astype(BF16)
        qg = jnp.tile(q_norm[l], N_HEADS)[None, :]
        kg = jnp.tile(k_norm[l], N_HEADS)[None, :]
        lora_pad = jnp.zeros((LANES // 2, GROUP_W), F32)
        prm = dict(mu=row(mu_shift), w0=row(w0), a0=row(a0),
                   w2p=jnp.concatenate([w2[l], lora_pad], axis=0).astype(BF16),
                   a2p=jnp.concatenate([lora_pad, a2[l]], axis=0).astype(BF16),
                   g2b=g2[l].astype(BF16), k_k=row(k_k), k_a=row(k_a), r_k=row(r_k), gsum=gsum)
        ffn = (row(gn_w), row(gn_b), gmean, w_o[l].astype(BF16), row(norm2),
               w_gate[l].astype(BF16), w_up[l].astype(BF16), w_down[l].astype(BF16))

        tm = min(TOKEN_TILE, bp * sp)
        q_b, k_p, k_b, v_p, v_b, pr = _inproj(yp, row(norm1), w_in_b, qg, kg, gmean, tm)
        seq = lambda a: a.reshape(bp, sp, a.shape[-1])
        att = _sb_prompt(seq(q_b), seq(k_b), seq(v_b), sb_bias[l], min(ATT_TILE, sp))
        pr = seq(pr)
        o, g, bonus, s_p = _rwkv_group(pr, jnp.zeros((bp, RWKV_PROJ_W), F32), None, prm,
                                       PREP_CHUNK, min(REC_CHUNK, sp))
        flat = lambda a: a.reshape(bp * sp, GROUP_W)
        yp = _out_ffn(yp, flat(att), flat(o), flat(g), flat(bonus), *ffn, tm)
        outs[0].append(k_p.reshape(bp, sp, N_HEADS, HEAD_DIM))
        outs[1].append(v_p.reshape(bp, sp, N_HEADS, HEAD_DIM))
        outs[2].append(s_p)
        outs[3].append(pr[:, -1])

        tm = min(TOKEN_TILE, bs * ts)
        q_b, k_s, _, v_s, _, pr = _inproj(ys, row(norm1), w_in_b, qg, kg, gmean, tm)
        seq = lambda a: a.reshape(bs, ts, a.shape[-1])
        n_pool, page = cache_k.shape[1], cache_k.shape[2]
        pool = lambda c: c[l].transpose(0, 2, 3, 1).reshape(n_pool, GROUP_W, page)
        att = _sb_decode(seq(q_b), k_s.reshape(bs, ts, GROUP_W), v_s.reshape(bs, ts, GROUP_W), sb_bias[l],
                         pool(cache_k), pool(cache_v), page_table,
                         min(PAGES_PER_STEP, page_table.shape[1]))
        pr = seq(pr)
        o, g, bonus, s_s = _rwkv_group(pr, state_shift[l], state_rwkv[l], prm,
                                       PREP_CHUNK, min(REC_CHUNK, ts))
        flat = lambda a: a.reshape(bs * ts, GROUP_W)
        ys = _out_ffn(ys, flat(att), flat(o), flat(g), flat(bonus), *ffn, tm)
        outs[4].append(k_s.reshape(bs, ts, N_HEADS, HEAD_DIM))
        outs[5].append(v_s.reshape(bs, ts, N_HEADS, HEAD_DIM))
        outs[6].append(s_s)
        outs[7].append(pr[:, -1])
    stacked = [jnp.stack(o) for o in outs]
    return (yp.reshape(bp, sp, d), ys.reshape(bs, ts, d), *stacked)
```

```python
import functools

import jax
import jax.numpy as jnp
from jax import lax
from jax.experimental import pallas as pl
from jax.experimental.pallas import tpu as pltpu

F32 = jnp.float32
BF16 = jnp.bfloat16

HEAD_DIM = 64
N_HEADS = 8
GROUP_W = N_HEADS * HEAD_DIM
LORA_W = 256
RWKV_PROJ_W = 3 * GROUP_W + LORA_W
SB_SCALE = HEAD_DIM ** -0.5
LOG2E = 1.4426950408889634
NORM_EPS = 1e-6
GN_EPS = 64e-5
LANES = 128
VMEM_LIMIT = 56 * 1024 * 1024
TOKEN_TILE = 256
ATT_TILE = 256
PREP_CHUNK = 256
REC_CHUNK = 32
SPREAD_UNROLL = 8
PAGES_PER_STEP = 8
PAGE_SLOTS = 3


def _dot(a, b):
    return jnp.dot(a, b, preferred_element_type=F32)


def _dot_nt(a, b):
    return lax.dot_general(a, b, (((1,), (1,)), ((), ())), preferred_element_type=F32)


def _softplus(y):
    return jnp.maximum(y, 0.0) + jnp.log(1.0 + jnp.exp2(jnp.abs(y) * -LOG2E))


def _sigmoid(y):
    return 1.0 / (1.0 + jnp.exp(-y))


def _params(sem):
    return pltpu.CompilerParams(dimension_semantics=sem, vmem_limit_bytes=VMEM_LIMIT)


def _const_spec(shape):
    nd = len(shape)
    return pl.BlockSpec(shape, lambda *_: (0,) * nd)


def _inproj_body(x_ref, n1_ref, w_ref, qg_ref, kg_ref, gm_ref,
                 q_ref, k_ref, kb_ref, v_ref, vb_ref, pr_ref):
    x = x_ref[...]
    xn = x * lax.rsqrt(jnp.mean(x * x, axis=-1, keepdims=True) + NORM_EPS) * n1_ref[...]
    xb = xn.astype(BF16)

    def head_norm(t, g):
        ms = _dot((t * t).astype(BF16), gm_ref[...])
        return t * lax.rsqrt(ms + NORM_EPS) * g

    q = _dot(xb, w_ref[:, 0:GROUP_W])
    q_ref[...] = (head_norm(q, qg_ref[...]) * SB_SCALE).astype(BF16)
    k = head_norm(_dot(xb, w_ref[:, GROUP_W:2 * GROUP_W]), kg_ref[...])
    kb_ref[...] = k.astype(BF16)
    v = _dot(xb, w_ref[:, 2 * GROUP_W:3 * GROUP_W])
    vb_ref[...] = v.astype(BF16)
    for h in range(N_HEADS):
        k_ref[:, h, :] = k[:, h * HEAD_DIM:(h + 1) * HEAD_DIM]
        v_ref[:, h, :] = v[:, h * HEAD_DIM:(h + 1) * HEAD_DIM]
    pr_ref[...] = _dot(xb, w_ref[:, 3 * GROUP_W:])


def _inproj(x, n1, w_in_b, qg, kg, gmean, tm):
    n, d = x.shape
    pw = w_in_b.shape[1]
    tok = lambda w: pl.BlockSpec((tm, w), lambda i: (i, 0))
    heads = pl.BlockSpec((tm, N_HEADS, HEAD_DIM), lambda i: (i, 0, 0))
    return pl.pallas_call(
        _inproj_body,
        grid=(n // tm,),
        in_specs=[tok(d), _const_spec((1, d)), _const_spec((d, pw)),
                  _const_spec((1, GROUP_W)), _const_spec((1, GROUP_W)),
                  _const_spec((GROUP_W, GROUP_W))],
        out_specs=[tok(GROUP_W), heads, tok(GROUP_W), heads, tok(GROUP_W),
                   tok(RWKV_PROJ_W)],
        out_shape=[jax.ShapeDtypeStruct((n, GROUP_W), BF16),
                   jax.ShapeDtypeStruct((n, N_HEADS, HEAD_DIM), F32),
                   jax.ShapeDtypeStruct((n, GROUP_W), BF16),
                   jax.ShapeDtypeStruct((n, N_HEADS, HEAD_DIM), F32),
                   jax.ShapeDtypeStruct((n, GROUP_W), BF16),
                   jax.ShapeDtypeStruct((n, RWKV_PROJ_W), F32)],
        compiler_params=_params(("parallel",)),
        name="inproj",
    )(x, n1, w_in_b, qg, kg, gmean)


def _sb_weights(z, neg_u, carry, mask):
    sp = _softplus(z)
    if mask is not None:
        sp = jnp.where(mask, sp, 0.0)
    suffix = _dot(sp.astype(BF16), neg_u)
    weight = jnp.exp2(((z - sp) + suffix + carry) * LOG2E)
    if mask is not None:
        weight = jnp.where(mask, weight, 0.0)
    return weight.astype(BF16), carry - jnp.sum(sp, axis=1, keepdims=True)


def _neg_suffix_matrix(tk):
    j = lax.broadcasted_iota(jnp.int32, (tk, tk), 0)
    s = lax.broadcasted_iota(jnp.int32, (tk, tk), 1)
    return jnp.where(j > s, -1.0, 0.0).astype(BF16)


def _sb_prompt_body(bias_ref, q_ref, k_ref, v_ref, u_ref, o_ref, *, tq):
    hp = pl.program_id(1)
    qi = pl.program_id(2)
    u = u_ref[...]
    q = q_ref[0]
    lane_head = lax.broadcasted_iota(jnp.int32, (tq, LANES), 1) // HEAD_DIM
    zero = jnp.zeros_like(q)
    qs = jnp.concatenate([jnp.where(lane_head == 0, q, zero),
                          jnp.where(lane_head == 1, q, zero)], axis=0)
    bias0 = bias_ref[hp * 2]
    bias1 = bias_ref[hp * 2 + 1]
    qpos = lax.broadcasted_iota(jnp.int32, (tq, tq), 0)
    kpos = lax.broadcasted_iota(jnp.int32, (tq, tq), 1)
    causal = jnp.concatenate([kpos < qpos] * 2, axis=0)

    def logits(kb):
        start = pl.multiple_of(kb * tq, tq)
        zz = _dot_nt(qs, k_ref[0, pl.ds(start, tq), :])
        return jnp.concatenate([zz[:tq] + bias0, zz[tq:] + bias1], axis=0)

    def values(kb, w):
        start = pl.multiple_of(kb * tq, tq)
        return _dot(w, v_ref[0, pl.ds(start, tq), :])

    w, carry = _sb_weights(logits(qi), u, jnp.zeros((2 * tq, 1), F32), causal)

    def body(i, state):
        z, w_newer, carry, acc = state
        kb = qi - 1 - i
        z_older = logits(jnp.maximum(kb - 1, 0))
        out_newer = values(kb + 1, w_newer)
        w, carry = _sb_weights(z, u, carry, None)
        return z_older, w, carry, acc + out_newer

    state = (logits(jnp.maximum(qi - 1, 0)), w, carry, jnp.zeros((2 * tq, LANES), F32))
    _, w, _, acc = lax.fori_loop(0, qi, body, state)
    acc = acc + values(0, w)
    o_ref[0] = jnp.where(lane_head == 0, acc[:tq], acc[tq:])


def _sb_prompt(q_b, k_b, v_b, bias, tq):
    b, s, _ = q_b.shape
    hpairs = GROUP_W // LANES
    return pl.pallas_call(
        functools.partial(_sb_prompt_body, tq=tq),
        grid=(b, hpairs, s // tq),
        in_specs=[pl.BlockSpec(memory_space=pltpu.SMEM),
                  pl.BlockSpec((1, tq, LANES), lambda bi, hp, qi: (bi, qi, hp)),
                  pl.BlockSpec((1, s, LANES), lambda bi, hp, qi: (bi, 0, hp)),
                  pl.BlockSpec((1, s, LANES), lambda bi, hp, qi: (bi, 0, hp)),
                  _const_spec((tq, tq))],
        out_specs=pl.BlockSpec((1, tq, LANES), lambda bi, hp, qi: (bi, qi, hp)),
        out_shape=jax.ShapeDtypeStruct((b, s, GROUP_W), F32),
        compiler_params=_params(("parallel", "parallel", "arbitrary")),
        name="sb_prompt",
    )(bias, q_b, k_b, v_b, _neg_suffix_matrix(tq))


def _sb_decode_body(pt_ref, qbd_ref, bias_ref, knew_ref, vnew_ref, u_ref, ck_hbm, cv_hbm, o_ref,
                    acc_ref, carry_ref, kbuf, vbuf, sem, *, pps, t_new, n_pages, side_work=None):
    jj = pl.program_id(1)
    steps_per_seq = pl.num_programs(1)
    step = pl.program_id(0) * steps_per_seq + jj
    n_steps = pl.num_programs(0) * steps_per_seq
    qbd = qbd_ref[0]
    bias = bias_ref[...]
    u = u_ref[...]
    m, page = bias.shape

    def page_copies(s):
        slot = s % PAGE_SLOTS
        first = (s // steps_per_seq) * n_pages + (n_pages - 1) - (s % steps_per_seq) * pps
        copies = []
        for i in range(pps):
            pid = pt_ref[first - i]
            copies.append(pltpu.make_async_copy(ck_hbm.at[pid], kbuf.at[slot, i], sem.at[slot, 0, i]))
            copies.append(pltpu.make_async_copy(cv_hbm.at[pid], vbuf.at[slot, i], sem.at[slot, 1, i]))
        return copies

    @pl.when(step == 0)
    def _():
        for s in range(PAGE_SLOTS - 1):
            @pl.when(s < n_steps)
            def _(s=s):
                for c in page_copies(s):
                    c.start()

    @pl.when(step + (PAGE_SLOTS - 1) < n_steps)
    def _():
        for c in page_copies(step + (PAGE_SLOTS - 1)):
            c.start()

    for c in page_copies(step):
        c.wait()
    slot = step % PAGE_SLOTS
    k_refs = [kbuf.at[slot, i] for i in range(pps)]
    v_refs = [vbuf.at[slot, i] for i in range(pps)]

    @pl.when(jj == 0)
    def _():
        pad = jnp.zeros((page - knew_ref.shape[1], GROUP_W), F32)
        kb = jnp.concatenate([knew_ref[0], pad], axis=0).astype(BF16)
        vb = jnp.concatenate([vnew_ref[0], pad], axis=0).astype(BF16)
        qpos = lax.broadcasted_iota(jnp.int32, (m, page), 0) // N_HEADS
        kpos = lax.broadcasted_iota(jnp.int32, (m, page), 1)
        w, carry = _sb_weights(_dot_nt(qbd, kb) + bias, u, jnp.zeros((m, page), F32), kpos < qpos)
        acc_ref[...] = _dot(w, vb)
        carry_ref[...] = carry

    z = jnp.concatenate([_dot(qbd, k_refs[i][...].astype(BF16)) + bias for i in range(pps)], axis=0)
    sp = _softplus(z)
    suffix = _dot(sp.astype(BF16), u)
    sp_sum = jnp.sum(sp, axis=1, keepdims=True)
    carry = carry_ref[...]
    carries = []
    for i in range(pps):
        carries.append(carry)
        carry = carry - sp_sum[i * m:(i + 1) * m]
    carry_ref[...] = carry
    weight = jnp.exp2(((z - sp) + suffix + jnp.concatenate(carries, axis=0)) * LOG2E).astype(BF16)
    acc = acc_ref[...]
    for i in range(pps):
        acc = acc + _dot_nt(weight[i * m:(i + 1) * m], v_refs[i][...].astype(BF16))
    acc_ref[...] = acc
    if side_work is not None:
        side_work()

    @pl.when(jj == pl.num_programs(1) - 1)
    def _():
        row_head = lax.broadcasted_iota(jnp.int32, (m, GROUP_W), 0) % N_HEADS
        lane_head = lax.broadcasted_iota(jnp.int32, (m, GROUP_W), 1) // HEAD_DIM
        own = jnp.where(row_head == lane_head, acc, 0.0)
        o_ref[0] = jnp.sum(own.reshape(t_new, N_HEADS, GROUP_W), axis=1)


def _sb_decode(q_b, k_new, v_new, bias, cache_k, cache_v, page_table, pps, rec=None):
    nb, t_new, _ = q_b.shape
    n_pages = page_table.shape[1]
    page = cache_k.shape[2]
    m = t_new * N_HEADS
    t_pad = -(-t_new // 8) * 8
    lane_head = jnp.arange(GROUP_W) // HEAD_DIM
    sel = (lane_head[None, :] == jnp.arange(N_HEADS)[:, None])
    qbd = jnp.where(sel[None, None], q_b[:, :, None, :], jnp.zeros((), BF16)).reshape(nb, m, GROUP_W)
    bias_tile = jnp.broadcast_to(jnp.tile(bias, t_new)[:, None], (m, page)).astype(F32)
    padn = ((0, 0), (0, t_pad - t_new), (0, 0))
    k_new = jnp.pad(k_new, padn)
    v_new = jnp.pad(v_new, padn)

    seq = lambda r, w: pl.BlockSpec((1, r, w), lambda b, jj, pt: (b, 0, 0))
    steps_per_seq = n_pages // pps
    in_specs = [seq(m, GROUP_W),
                pl.BlockSpec((m, page), lambda b, jj, pt: (0, 0)),
                seq(t_pad, GROUP_W), seq(t_pad, GROUP_W),
                pl.BlockSpec((page, page), lambda b, jj, pt: (0, 0)),
                pl.BlockSpec(memory_space=pl.ANY), pl.BlockSpec(memory_space=pl.ANY)]
    out_specs = [seq(t_new, GROUP_W)]
    out_shape = [jax.ShapeDtypeStruct((nb, t_new, GROUP_W), F32)]
    scratch = [pltpu.VMEM((m, GROUP_W), F32), pltpu.VMEM((m, page), F32),
               pltpu.VMEM((PAGE_SLOTS, pps, GROUP_W, page), F32),
               pltpu.VMEM((PAGE_SLOTS, pps, GROUP_W, page), F32),
               pltpu.SemaphoreType.DMA((PAGE_SLOTS, 2, pps))]
    operands = [qbd, bias_tile, k_new, v_new, _neg_suffix_matrix(page), cache_k, cache_v]
    body = functools.partial(_sb_decode_body, pps=pps, t_new=t_new, n_pages=n_pages)
    if rec is not None:
        t_rec, nk, lanes = rec[0].shape
        nv = rec[5].shape[1]
        per_seq = t_rec // nb
        assert nk == HEAD_DIM // 2 and lanes == LANES and t_rec == per_seq * nb
        assert per_seq % steps_per_seq == 0
        steps = lambda rows: pl.BlockSpec((per_seq, rows, LANES), lambda b, jj, pt: (b, 0, 0))
        state = pl.BlockSpec((1, nv, HEAD_DIM, LANES), lambda b, jj, pt: (0, 0, 0, 0))
        in_specs += [steps(nk)] * 5 + [steps(nv), state]
        out_specs += [steps(nv), state]
        out_shape += [jax.ShapeDtypeStruct((t_rec, nv, LANES), F32),
                      jax.ShapeDtypeStruct(rec[6].shape, F32)]
        scratch.append(pltpu.VMEM((nv, HEAD_DIM, LANES), F32))
        operands += list(rec)
        body = functools.partial(_decode_rec_body, pps=pps, t_new=t_new, n_pages=n_pages,
                                 rec_steps=per_seq // steps_per_seq)
    grid_spec = pltpu.PrefetchScalarGridSpec(
        num_scalar_prefetch=1, grid=(nb, steps_per_seq),
        in_specs=in_specs, out_specs=out_specs, scratch_shapes=scratch)
    outs = pl.pallas_call(
        body,
        grid_spec=grid_spec,
        out_shape=out_shape,
        compiler_params=_params(("arbitrary", "arbitrary")),
        name="sb_decode",
    )(page_table.reshape(-1), *operands)
    return outs if rec is not None else outs[0]


def _rwkv_prep_body(pr_ref, sh_ref, mu_ref, w0_ref, a0_ref, w2_ref, a2_ref, g2_ref,
                    kk_ref, ka_ref, rk_ref, gs_ref,
                    w_o, nkk_o, b_o, km_o, r_o, v_o, g_o, bonus_o, prev_ref, *, seq_len):
    pr = pr_ref[0]
    tc = pr.shape[0]
    row = lax.broadcasted_iota(jnp.int32, pr.shape, 0)
    rolled = pltpu.roll(pr, 1, axis=0)
    if seq_len is None:
        @pl.when(pl.program_id(1) == 0)
        def _():
            prev_ref[...] = sh_ref[0]

        prev = jnp.where(row == 0, prev_ref[...], rolled)
        prev_ref[...] = pr[tc - 1:tc, :]
    else:
        prev = jnp.where(row % seq_len == 0, sh_ref[0], rolled)
    xs = pr + (prev - pr) * mu_ref[...]
    r = xs[:, 0:GROUP_W]
    k = xs[:, GROUP_W:2 * GROUP_W]
    v = xs[:, 2 * GROUP_W:3 * GROUP_W]
    wa = xs[:, 3 * GROUP_W:3 * GROUP_W + LANES]
    gd = xs[:, 3 * GROUP_W + LANES:]
    gs = gs_ref[...]
    w_raw = -_softplus(-(w0_ref[...] + _dot(jnp.tanh(wa).astype(BF16), w2_ref[...]))) - 0.5
    w_o[0] = jnp.exp(-jnp.exp(w_raw))
    a = _sigmoid(a0_ref[...] + _dot(wa.astype(BF16), a2_ref[...]))
    g_o[0] = _dot(_sigmoid(gd).astype(BF16), g2_ref[...])
    kk = k * kk_ref[...]
    ss = _dot((kk * kk).astype(BF16), gs)
    kk = kk / jnp.maximum(jnp.sqrt(ss), 1e-12)
    nkk_o[0] = -kk
    b_o[0] = kk * a
    km = k * (1.0 + (a - 1.0) * ka_ref[...])
    km_o[0] = km
    r_o[0] = r
    v_o[0] = v
    bonus_o[0] = _dot((r * km * rk_ref[...]).astype(BF16), gs) * v


def _rwkv_prep(pr, shift, seq_len, mu, w0, a0, w2p, a2p, g2b, k_k, k_a, r_k, gsum, tc):
    b, s, pw = pr.shape
    blk = pl.BlockSpec((1, tc, GROUP_W), lambda bi, c: (bi, c, 0))
    vec = _const_spec((1, GROUP_W))
    if seq_len is None:
        shift_spec = pl.BlockSpec((1, 1, pw), lambda bi, c: (bi, 0, 0))
    else:
        assert tc % seq_len == 0
        shift_spec = pl.BlockSpec((1, tc, pw), lambda bi, c: (bi, c, 0))
    return pl.pallas_call(
        functools.partial(_rwkv_prep_body, seq_len=seq_len),
        grid=(b, s // tc),
        in_specs=[pl.BlockSpec((1, tc, pw), lambda bi, c: (bi, c, 0)),
                  shift_spec,
                  _const_spec((1, pw)), vec, vec,
                  _const_spec((LANES, GROUP_W)), _const_spec((LANES, GROUP_W)),
                  _const_spec((LANES, GROUP_W)), vec, vec, vec,
                  _const_spec((GROUP_W, GROUP_W))],
        out_specs=[blk] * 8,
        out_shape=[jax.ShapeDtypeStruct((b, s, GROUP_W), F32)] * 8,
        scratch_shapes=[pltpu.VMEM((1, pw), F32)],
        compiler_params=_params(("parallel", "arbitrary")),
        name="rwkv_prep",
    )(pr, shift, mu, w0, a0, w2p, a2p, g2b, k_k, k_a, r_k, gsum)


def _spread_keys(x):
    low = lax.broadcasted_iota(jnp.int32, x.shape, 1) < LANES // 2
    other = pltpu.roll(x, LANES // 2, axis=1)
    return jnp.concatenate([jnp.where(low, x, other), jnp.where(low, other, x)], axis=0)


def _rec_step(keyed, v_ref, o_ref, s_ref):
    w, nkk, b, km, r = keyed
    for vi in range(s_ref.shape[0]):
        s = s_ref[vi]
        sa = jnp.sum(s * nkk, axis=0, keepdims=True)
        s = s * w + sa * b + v_ref[pl.ds(vi, 1), :] * km
        s_ref[vi] = s
        o_ref[pl.ds(vi, 1), :] = jnp.sum(s * r, axis=0, keepdims=True)


def _rwkv_rec_body(w_ref, nkk_ref, b_ref, km_ref, r_ref, v_ref, s0_ref, o_ref, st_ref, s_ref,
                   *spread_ref, tc):
    c = pl.program_id(1)

    @pl.when(c == 0)
    def _():
        s_ref[...] = s0_ref[0]

    keyed = (w_ref, nkk_ref, b_ref, km_ref, r_ref)
    if spread_ref:
        def spread(i, carry):
            for dt in range(SPREAD_UNROLL):
                t = i * SPREAD_UNROLL + dt
                for j, ref in enumerate(keyed):
                    spread_ref[0][j, t] = _spread_keys(ref[t])
            return carry

        lax.fori_loop(0, tc // SPREAD_UNROLL, spread, 0)
        keyed = tuple(spread_ref[0].at[j] for j in range(len(keyed)))

    def step(t, carry):
        _rec_step(tuple(ref[t] for ref in keyed), v_ref.at[t], o_ref.at[t], s_ref)
        return carry

    lax.fori_loop(0, tc, step, 0)

    @pl.when(c == pl.num_programs(1) - 1)
    def _():
        st_ref[0] = s_ref[...]


def _rwkv_rec(w, nkk, b, km, r, v, s0, tc):
    t, nk, l = w.shape
    nv = v.shape[1]
    kvec = pl.BlockSpec((tc, nk, LANES), lambda g, c: (c, 0, g))
    vvec = pl.BlockSpec((tc, nv, LANES), lambda g, c: (c, 0, g))
    st = pl.BlockSpec((1, nv, HEAD_DIM, LANES), lambda g, c: (g, 0, 0, 0))
    scratch = [pltpu.VMEM((nv, HEAD_DIM, LANES), F32)]
    if nk < HEAD_DIM:
        assert tc % SPREAD_UNROLL == 0
        scratch.append(pltpu.VMEM((5, tc, HEAD_DIM, LANES), F32))
    return pl.pallas_call(
        functools.partial(_rwkv_rec_body, tc=tc),
        grid=(l // LANES, t // tc),
        in_specs=[kvec] * 5 + [vvec, st],
        out_specs=[vvec, st],
        out_shape=[jax.ShapeDtypeStruct((t, nv, l), F32),
                   jax.ShapeDtypeStruct(s0.shape, F32)],
        scratch_shapes=scratch,
        compiler_params=_params(("parallel", "arbitrary")),
        name="rwkv_rec",
    )(w, nkk, b, km, r, v, s0)


def _decode_rec_body(pt_ref, qbd_ref, bias_ref, knew_ref, vnew_ref, u_ref, ck_hbm, cv_hbm,
                     w_ref, nkk_ref, b_ref, km_ref, r_ref, v_ref, s0_ref,
                     o_ref, orec_ref, st_ref,
                     acc_ref, carry_ref, kbuf, vbuf, sem, s_ref, *, rec_steps, **decode_args):
    jj = pl.program_id(1)
    step = pl.program_id(0) * pl.num_programs(1) + jj

    @pl.when(step == 0)
    def _():
        s_ref[...] = s0_ref[0]

    def recurrence():
        for dt in range(rec_steps):
            t = jj * rec_steps + dt
            keyed = tuple(_spread_keys(ref[t]) for ref in (w_ref, nkk_ref, b_ref, km_ref, r_ref))
            _rec_step(keyed, v_ref.at[t], orec_ref.at[t], s_ref)

    _sb_decode_body(pt_ref, qbd_ref, bias_ref, knew_ref, vnew_ref, u_ref, ck_hbm, cv_hbm, o_ref,
                    acc_ref, carry_ref, kbuf, vbuf, sem, side_work=recurrence, **decode_args)

    @pl.when(step == pl.num_programs(0) * pl.num_programs(1) - 1)
    def _():
        st_ref[0] = s_ref[...]


def _out_ffn_body(x_ref, att_ref, o_ref, g_ref, bonus_ref, gnw_ref, gnb_ref, gm_ref,
                  wo_ref, n2_ref, wg_ref, wu_ref, wd_ref, y_ref):
    gm = gm_ref[...]
    o = o_ref[...]
    mean = _dot(o.astype(BF16), gm)
    d = o - mean
    var = _dot((d * d).astype(BF16), gm)
    rw = (d * lax.rsqrt(var + GN_EPS) * gnw_ref[...] + gnb_ref[...] + bonus_ref[...]) * g_ref[...]
    h = (x_ref[...] + _dot(att_ref[...].astype(BF16), wo_ref[0:GROUP_W, :])
         + _dot(rw.astype(BF16), wo_ref[GROUP_W:, :]))
    hn = h * lax.rsqrt(jnp.mean(h * h, axis=-1, keepdims=True) + NORM_EPS) * n2_ref[...]
    hb = hn.astype(BF16)
    gate = _dot(hb, wg_ref[...])
    act = gate * _sigmoid(gate) * _dot(hb, wu_ref[...])
    y_ref[...] = h + _dot(act.astype(BF16), wd_ref[...])


def _out_ffn(x, att, o, g, bonus, gn_w, gn_b, gmean, wo_b, n2, wg_b, wu_b, wd_b, tm):
    n, d = x.shape
    dff = wg_b.shape[1]
    tok = lambda w: pl.BlockSpec((tm, w), lambda i: (i, 0))
    once = lambda shape: pl.BlockSpec(shape, lambda i: (0, 0), pipeline_mode=pl.Buffered(1))
    return pl.pallas_call(
        _out_ffn_body,
        grid=(n // tm,),
        in_specs=[tok(d), tok(GROUP_W), tok(GROUP_W), tok(GROUP_W), tok(GROUP_W),
                  once((1, GROUP_W)), once((1, GROUP_W)), once((GROUP_W, GROUP_W)),
                  once((2 * GROUP_W, d)), once((1, d)),
                  once((d, dff)), once((d, dff)), once((dff, d))],
        out_specs=tok(d),
        out_shape=jax.ShapeDtypeStruct((n, d), F32),
        compiler_params=_params(("parallel",)),
        name="out_ffn",
    )(x, att, o, g, bonus, gn_w, gn_b, gmean, wo_b, n2, wg_b, wu_b, wd_b)


def _to_chain_lanes(x, halves):
    b, t, _ = x.shape
    if halves:
        y = x.reshape(b, t, N_HEADS, 2, HEAD_DIM // 2).transpose(1, 4, 3, 0, 2)
        return y.reshape(t, HEAD_DIM // 2, 2 * b * N_HEADS)
    y = x.reshape(b, t, N_HEADS, HEAD_DIM).transpose(1, 3, 2, 0)
    return y.reshape(t, HEAD_DIM, N_HEADS * b)


def _from_chain_lanes(o, b, halves):
    t = o.shape[0]
    if halves:
        y = o.reshape(t, HEAD_DIM // 2, 2, b, N_HEADS).transpose(3, 0, 4, 2, 1)
    else:
        y = o.reshape(t, HEAD_DIM, N_HEADS, b).transpose(3, 0, 2, 1)
    return y.reshape(b, t, GROUP_W)


def _state_to_chain_lanes(s):
    b = s.shape[0]
    y = s.transpose(1, 2, 3, 0)
    if b % LANES == 0:
        y = y.reshape(N_HEADS, HEAD_DIM, HEAD_DIM, b // LANES, LANES).transpose(0, 3, 1, 2, 4)
    else:
        y = y.transpose(1, 2, 0, 3)
    return y.reshape(N_HEADS * b // LANES, HEAD_DIM, HEAD_DIM, LANES)


def _state_from_chain_lanes(s, b, halves):
    if halves:
        y = s.reshape(HEAD_DIM // 2, HEAD_DIM, 2, b, N_HEADS).transpose(3, 4, 2, 0, 1)
    elif b % LANES == 0:
        y = s.reshape(N_HEADS, b // LANES, HEAD_DIM, HEAD_DIM, LANES).transpose(1, 4, 0, 2, 3)
    else:
        y = s.reshape(HEAD_DIM, HEAD_DIM, N_HEADS, b).transpose(3, 2, 0, 1)
    return y.reshape(b, N_HEADS, HEAD_DIM, HEAD_DIM)


def _rwkv_operands(pr, shift_prev, s0, prm, tc_prep):
    b, t, pw = pr.shape
    weights = (prm["mu"], prm["w0"], prm["a0"], prm["w2p"], prm["a2p"], prm["g2b"],
               prm["k_k"], prm["k_a"], prm["r_k"], prm["gsum"])
    if t >= tc_prep:
        vecs = _rwkv_prep(pr, shift_prev[:, None, :], None, *weights, tc_prep)
    else:
        starts = jnp.zeros((b, t, pw), F32).at[:, 0, :].set(shift_prev)
        vecs = _rwkv_prep(pr.reshape(1, b * t, pw), starts.reshape(1, b * t, pw), t,
                          *weights, min(tc_prep, b * t))
        vecs = [a.reshape(b, t, GROUP_W) for a in vecs]
    w, nkk, bv, km, r, v, g, bonus = vecs
    halves = 2 * b * N_HEADS == LANES
    vecs = [_to_chain_lanes(a, halves) for a in (w, nkk, bv, km, r, v)]
    if halves:
        assert s0 is None
        s0c = jnp.zeros((1, HEAD_DIM // 2, HEAD_DIM, LANES), F32)
    else:
        s0c = _state_to_chain_lanes(s0)
    return (*vecs, s0c), g, bonus, halves


def _group_matrix(scale):
    i = jnp.arange(GROUP_W) // HEAD_DIM
    return jnp.where(i[:, None] == i[None, :], scale, 0.0).astype(BF16)


def kernel(x_prompt, x_sample, cache_k, cache_v, state_rwkv, state_shift, page_table, w_in, q_norm, k_norm, sb_bias, mu_shift, w0, w2, a0, a2, g2, k_k, k_a, r_k, gn_w, gn_b, w_o, norm1, norm2, w_gate, w_up, w_down):
    depth = w_in.shape[0]
    bp, sp, d = x_prompt.shape
    bs, ts, _ = x_sample.shape
    gmean = _group_matrix(1.0 / HEAD_DIM)
    gsum = _group_matrix(1.0)
    yp = x_prompt.reshape(bp * sp, d)
    ys = x_sample.reshape(bs * ts, d)
    outs = [[] for _ in range(8)]
    for l in range(depth):
        row = lambda a: a[l][None, :]
        w_in_b = w_in[l].astype(BF16)
        qg = jnp.tile(q_norm[l], N_HEADS)[None, :]
        kg = jnp.tile(k_norm[l], N_HEADS)[None, :]
        lora_pad = jnp.zeros((LANES // 2, GROUP_W), F32)
        prm = dict(mu=row(mu_shift), w0=row(w0), a0=row(a0),
                   w2p=jnp.concatenate([w2[l], lora_pad], axis=0).astype(BF16),
                   a2p=jnp.concatenate([lora_pad, a2[l]], axis=0).astype(BF16),
                   g2b=g2[l].astype(BF16), k_k=row(k_k), k_a=row(k_a), r_k=row(r_k), gsum=gsum)
        ffn = (row(gn_w), row(gn_b), gmean, w_o[l].astype(BF16), row(norm2),
               w_gate[l].astype(BF16), w_up[l].astype(BF16), w_down[l].astype(BF16))

        tm_p = min(TOKEN_TILE, bp * sp)
        tm_s = min(TOKEN_TILE, bs * ts)
        q_p, k_p, kb_p, v_p, vb_p, pr_p = _inproj(yp, row(norm1), w_in_b, qg, kg, gmean, tm_p)
        q_s, k_s, _, v_s, _, pr_s = _inproj(ys, row(norm1), w_in_b, qg, kg, gmean, tm_s)
        pr_p = pr_p.reshape(bp, sp, RWKV_PROJ_W)
        pr_s = pr_s.reshape(bs, ts, RWKV_PROJ_W)
        seq_p = lambda a: a.reshape(bp, sp, GROUP_W)
        seq_s = lambda a: a.reshape(bs, ts, GROUP_W)

        att_p = _sb_prompt(seq_p(q_p), seq_p(kb_p), seq_p(vb_p), sb_bias[l], min(ATT_TILE, sp))
        rec_p, g_p, bonus_p, halves_p = _rwkv_operands(
            pr_p, jnp.zeros((bp, RWKV_PROJ_W), F32), None, prm, PREP_CHUNK)
        rec_s, g_s, bonus_s, halves_s = _rwkv_operands(pr_s, state_shift[l], state_rwkv[l], prm,
                                                       PREP_CHUNK)
        n_pool, page = cache_k.shape[1], cache_k.shape[2]
        pool = lambda c: c[l].transpose(0, 2, 3, 1).reshape(n_pool, GROUP_W, page)
        pps = min(PAGES_PER_STEP, page_table.shape[1])
        decode = functools.partial(_sb_decode, seq_s(q_s), seq_s(k_s), seq_s(v_s), sb_bias[l],
                                   pool(cache_k), pool(cache_v), page_table, pps)
        decode_steps = bs * (page_table.shape[1] // pps)
        if halves_p and sp % decode_steps == 0:
            att_s, o_p, last_p = decode(rec=rec_p)
        else:
            att_s = decode()
            o_p, last_p = _rwkv_rec(*rec_p, min(REC_CHUNK, sp))
        o_s, last_s = _rwkv_rec(*rec_s, min(REC_CHUNK, ts))

        yp = _out_ffn(yp, seq_p(att_p).reshape(bp * sp, GROUP_W),
                      _from_chain_lanes(o_p, bp, halves_p).reshape(bp * sp, GROUP_W),
                      g_p.reshape(bp * sp, GROUP_W), bonus_p.reshape(bp * sp, GROUP_W), *ffn, tm_p)
        ys = _out_ffn(ys, att_s.reshape(bs * ts, GROUP_W),
                      _from_chain_lanes(o_s, bs, halves_s).reshape(bs * ts, GROUP_W),
                      g_s.reshape(bs * ts, GROUP_W), bonus_s.reshape(bs * ts, GROUP_W), *ffn, tm_s)
        outs[0].append(k_p.reshape(bp, sp, N_HEADS, HEAD_DIM))
        outs[1].append(v_p.reshape(bp, sp, N_HEADS, HEAD_DIM))
        outs[2].append(_state_from_chain_lanes(last_p, bp, halves_p))
        outs[3].append(pr_p[:, -1])
        outs[4].append(k_s.reshape(bs, ts, N_HEADS, HEAD_DIM))
        outs[5].append(v_s.reshape(bs, ts, N_HEADS, HEAD_DIM))
        outs[6].append(_state_from_chain_lanes(last_s, bs, halves_s))
        outs[7].append(pr_s[:, -1])
    stacked = [jnp.stack(o) for o in outs]
    return (yp.reshape(bp, sp, d), ys.reshape(bs, ts, d), *stacked)
```

```python
import functools

import jax
import jax.numpy as jnp
from jax import lax
from jax.experimental import pallas as pl
from jax.experimental.pallas import tpu as pltpu

F32 = jnp.float32
BF16 = jnp.bfloat16

HEAD_DIM = 64
N_HEADS = 8
GROUP_W = N_HEADS * HEAD_DIM
LORA_W = 256
RWKV_PROJ_W = 3 * GROUP_W + LORA_W
SB_SCALE = HEAD_DIM ** -0.5
LOG2E = 1.4426950408889634
NORM_EPS = 1e-6
GN_EPS = 64e-5
LANES = 128
VMEM_LIMIT = 56 * 1024 * 1024
TOKEN_TILE = 256
ATT_TILE = 256
PREP_CHUNK = 256
REC_CHUNK = 32
SPREAD_UNROLL = 8
PAGES_PER_STEP = 8
PAGE_SLOTS = 3


def _dot(a, b):
    return jnp.dot(a, b, preferred_element_type=F32)


def _dot_nt(a, b):
    return lax.dot_general(a, b, (((1,), (1,)), ((), ())), preferred_element_type=F32)


def _softplus(y):
    return jnp.maximum(y, 0.0) + jnp.log(1.0 + jnp.exp2(jnp.abs(y) * -LOG2E))


def _sigmoid(y):
    return 1.0 / (1.0 + jnp.exp(-y))


def _params(sem):
    return pltpu.CompilerParams(dimension_semantics=sem, vmem_limit_bytes=VMEM_LIMIT)


def _const_spec(shape):
    nd = len(shape)
    return pl.BlockSpec(shape, lambda *_: (0,) * nd)


def _inproj_body(x_ref, n1_ref, w_ref, qg_ref, kg_ref, gm_ref,
                 q_ref, k_ref, kb_ref, v_ref, vb_ref, pr_ref):
    x = x_ref[...]
    xn = x * lax.rsqrt(jnp.mean(x * x, axis=-1, keepdims=True) + NORM_EPS) * n1_ref[...]
    xb = xn.astype(BF16)

    def head_norm(t, g):
        ms = _dot((t * t).astype(BF16), gm_ref[...])
        return t * lax.rsqrt(ms + NORM_EPS) * g

    q = _dot(xb, w_ref[:, 0:GROUP_W])
    q_ref[...] = (head_norm(q, qg_ref[...]) * SB_SCALE).astype(BF16)
    k = head_norm(_dot(xb, w_ref[:, GROUP_W:2 * GROUP_W]), kg_ref[...])
    kb_ref[...] = k.astype(BF16)
    v = _dot(xb, w_ref[:, 2 * GROUP_W:3 * GROUP_W])
    vb_ref[...] = v.astype(BF16)
    for h in range(N_HEADS):
        k_ref[:, h, :] = k[:, h * HEAD_DIM:(h + 1) * HEAD_DIM]
        v_ref[:, h, :] = v[:, h * HEAD_DIM:(h + 1) * HEAD_DIM]
    pr_ref[...] = _dot(xb, w_ref[:, 3 * GROUP_W:])


def _inproj(x, n1, w_in_b, qg, kg, gmean, tm):
    n, d = x.shape
    pw = w_in_b.shape[1]
    tok = lambda w: pl.BlockSpec((tm, w), lambda i: (i, 0))
    heads = pl.BlockSpec((tm, N_HEADS, HEAD_DIM), lambda i: (i, 0, 0))
    return pl.pallas_call(
        _inproj_body,
        grid=(n // tm,),
        in_specs=[tok(d), _const_spec((1, d)), _const_spec((d, pw)),
                  _const_spec((1, GROUP_W)), _const_spec((1, GROUP_W)),
                  _const_spec((GROUP_W, GROUP_W))],
        out_specs=[tok(GROUP_W), heads, tok(GROUP_W), heads, tok(GROUP_W),
                   tok(RWKV_PROJ_W)],
        out_shape=[jax.ShapeDtypeStruct((n, GROUP_W), BF16),
                   jax.ShapeDtypeStruct((n, N_HEADS, HEAD_DIM), F32),
                   jax.ShapeDtypeStruct((n, GROUP_W), BF16),
                   jax.ShapeDtypeStruct((n, N_HEADS, HEAD_DIM), F32),
                   jax.ShapeDtypeStruct((n, GROUP_W), BF16),
                   jax.ShapeDtypeStruct((n, RWKV_PROJ_W), F32)],
        compiler_params=_params(("parallel",)),
        name="inproj",
    )(x, n1, w_in_b, qg, kg, gmean)


def _sb_weights(z, neg_u, carry, mask):
    sp = _softplus(z)
    if mask is not None:
        sp = jnp.where(mask, sp, 0.0)
    suffix = _dot(sp.astype(BF16), neg_u)
    weight = jnp.exp2(((z - sp) + suffix + carry) * LOG2E)
    if mask is not None:
        weight = jnp.where(mask, weight, 0.0)
    return weight.astype(BF16), carry - jnp.sum(sp, axis=1, keepdims=True)


def _neg_suffix_matrix(tk):
    j = lax.broadcasted_iota(jnp.int32, (tk, tk), 0)
    s = lax.broadcasted_iota(jnp.int32, (tk, tk), 1)
    return jnp.where(j > s, -1.0, 0.0).astype(BF16)


def _sb_prompt_body(bias_ref, q_ref, k_ref, v_ref, u_ref, o_ref, *, tq):
    hp = pl.program_id(1)
    qi = pl.program_id(2)
    u = u_ref[...]
    q = q_ref[0]
    lane_head = lax.broadcasted_iota(jnp.int32, (tq, LANES), 1) // HEAD_DIM
    zero = jnp.zeros_like(q)
    qs = jnp.concatenate([jnp.where(lane_head == 0, q, zero),
                          jnp.where(lane_head == 1, q, zero)], axis=0)
    bias0 = bias_ref[hp * 2]
    bias1 = bias_ref[hp * 2 + 1]
    qpos = lax.broadcasted_iota(jnp.int32, (tq, tq), 0)
    kpos = lax.broadcasted_iota(jnp.int32, (tq, tq), 1)
    causal = jnp.concatenate([kpos < qpos] * 2, axis=0)

    def logits(kb):
        start = pl.multiple_of(kb * tq, tq)
        zz = _dot_nt(qs, k_ref[0, pl.ds(start, tq), :])
        return jnp.concatenate([zz[:tq] + bias0, zz[tq:] + bias1], axis=0)

    def values(kb, w):
        start = pl.multiple_of(kb * tq, tq)
        return _dot(w, v_ref[0, pl.ds(start, tq), :])

    w, carry = _sb_weights(logits(qi), u, jnp.zeros((2 * tq, 1), F32), causal)

    def body(i, state):
        z, w_newer, carry, acc = state
        kb = qi - 1 - i
        z_older = logits(jnp.maximum(kb - 1, 0))
        out_newer = values(kb + 1, w_newer)
        w, carry = _sb_weights(z, u, carry, None)
        return z_older, w, carry, acc + out_newer

    state = (logits(jnp.maximum(qi - 1, 0)), w, carry, jnp.zeros((2 * tq, LANES), F32))
    _, w, _, acc = lax.fori_loop(0, qi, body, state)
    acc = acc + values(0, w)
    o_ref[0] = jnp.where(lane_head == 0, acc[:tq], acc[tq:])


def _sb_prompt(q_b, k_b, v_b, bias, tq):
    b, s, _ = q_b.shape
    hpairs = GROUP_W // LANES
    return pl.pallas_call(
        functools.partial(_sb_prompt_body, tq=tq),
        grid=(b, hpairs, s // tq),
        in_specs=[pl.BlockSpec(memory_space=pltpu.SMEM),
                  pl.BlockSpec((1, tq, LANES), lambda bi, hp, qi: (bi, qi, hp)),
                  pl.BlockSpec((1, s, LANES), lambda bi, hp, qi: (bi, 0, hp)),
                  pl.BlockSpec((1, s, LANES), lambda bi, hp, qi: (bi, 0, hp)),
                  _const_spec((tq, tq))],
        out_specs=pl.BlockSpec((1, tq, LANES), lambda bi, hp, qi: (bi, qi, hp)),
        out_shape=jax.ShapeDtypeStruct((b, s, GROUP_W), F32),
        compiler_params=_params(("parallel", "parallel", "arbitrary")),
        name="sb_prompt",
    )(bias, q_b, k_b, v_b, _neg_suffix_matrix(tq))


def _sb_decode_body(pt_ref, qbd_ref, bias_ref, knew_ref, vnew_ref, u_ref, ck_hbm, cv_hbm, o_ref,
                    acc_ref, carry_ref, kbuf, vbuf, sem, *, pps, t_new, n_pages, side_work=None):
    jj = pl.program_id(1)
    steps_per_seq = pl.num_programs(1)
    step = pl.program_id(0) * steps_per_seq + jj
    n_steps = pl.num_programs(0) * steps_per_seq
    qbd = qbd_ref[0]
    bias = bias_ref[...]
    u = u_ref[...]
    m, page = bias.shape

    def page_copies(s):
        slot = s % PAGE_SLOTS
        first = (s // steps_per_seq) * n_pages + (n_pages - 1) - (s % steps_per_seq) * pps
        copies = []
        for i in range(pps):
            pid = pt_ref[first - i]
            copies.append(pltpu.make_async_copy(ck_hbm.at[pid], kbuf.at[slot, i], sem.at[slot, 0, i]))
            copies.append(pltpu.make_async_copy(cv_hbm.at[pid], vbuf.at[slot, i], sem.at[slot, 1, i]))
        return copies

    @pl.when(step == 0)
    def _():
        for s in range(PAGE_SLOTS - 1):
            @pl.when(s < n_steps)
            def _(s=s):
                for c in page_copies(s):
                    c.start()

    @pl.when(step + (PAGE_SLOTS - 1) < n_steps)
    def _():
        for c in page_copies(step + (PAGE_SLOTS - 1)):
            c.start()

    for c in page_copies(step):
        c.wait()
    slot = step % PAGE_SLOTS
    k_refs = [kbuf.at[slot, i] for i in range(pps)]
    v_refs = [vbuf.at[slot, i] for i in range(pps)]

    @pl.when(jj == 0)
    def _():
        pad = jnp.zeros((page - knew_ref.shape[1], GROUP_W), F32)
        kb = jnp.concatenate([knew_ref[0], pad], axis=0).astype(BF16)
        vb = jnp.concatenate([vnew_ref[0], pad], axis=0).astype(BF16)
        qpos = lax.broadcasted_iota(jnp.int32, (m, page), 0) // N_HEADS
        kpos = lax.broadcasted_iota(jnp.int32, (m, page), 1)
        w, carry = _sb_weights(_dot_nt(qbd, kb) + bias, u, jnp.zeros((m, page), F32), kpos < qpos)
        acc_ref[...] = _dot(w, vb)
        carry_ref[...] = carry

    z = jnp.concatenate([_dot(qbd, k_refs[i][...].astype(BF16)) + bias for i in range(pps)], axis=0)
    sp = _softplus(z)
    suffix = _dot(sp.astype(BF16), u)
    sp_sum = jnp.sum(sp, axis=1, keepdims=True)
    carry = carry_ref[...]
    carries = []
    for i in range(pps):
        carries.append(carry)
        carry = carry - sp_sum[i * m:(i + 1) * m]
    carry_ref[...] = carry
    weight = jnp.exp2(((z - sp) + suffix + jnp.concatenate(carries, axis=0)) * LOG2E).astype(BF16)
    acc = acc_ref[...]
    for i in range(pps):
        acc = acc + _dot_nt(weight[i * m:(i + 1) * m], v_refs[i][...].astype(BF16))
    acc_ref[...] = acc
    if side_work is not None:
        side_work()

    @pl.when(jj == pl.num_programs(1) - 1)
    def _():
        row_head = lax.broadcasted_iota(jnp.int32, (m, GROUP_W), 0) % N_HEADS
        lane_head = lax.broadcasted_iota(jnp.int32, (m, GROUP_W), 1) // HEAD_DIM
        own = jnp.where(row_head == lane_head, acc, 0.0)
        o_ref[0] = jnp.sum(own.reshape(t_new, N_HEADS, GROUP_W), axis=1)


def _sb_decode(q_b, k_new, v_new, bias, cache_k, cache_v, page_table, pps, rec=None):
    nb, t_new, _ = q_b.shape
    n_pages = page_table.shape[1]
    page = cache_k.shape[2]
    m = t_new * N_HEADS
    t_pad = -(-t_new // 8) * 8
    lane_head = jnp.arange(GROUP_W) // HEAD_DIM
    sel = (lane_head[None, :] == jnp.arange(N_HEADS)[:, None])
    qbd = jnp.where(sel[None, None], q_b[:, :, None, :], jnp.zeros((), BF16)).reshape(nb, m, GROUP_W)
    bias_tile = jnp.broadcast_to(jnp.tile(bias, t_new)[:, None], (m, page)).astype(F32)
    padn = ((0, 0), (0, t_pad - t_new), (0, 0))
    k_new = jnp.pad(k_new, padn)
    v_new = jnp.pad(v_new, padn)

    seq = lambda r, w: pl.BlockSpec((1, r, w), lambda b, jj, pt: (b, 0, 0))
    steps_per_seq = n_pages // pps
    in_specs = [seq(m, GROUP_W),
                pl.BlockSpec((m, page), lambda b, jj, pt: (0, 0)),
                seq(t_pad, GROUP_W), seq(t_pad, GROUP_W),
                pl.BlockSpec((page, page), lambda b, jj, pt: (0, 0)),
                pl.BlockSpec(memory_space=pl.ANY), pl.BlockSpec(memory_space=pl.ANY)]
    out_specs = [seq(t_new, GROUP_W)]
    out_shape = [jax.ShapeDtypeStruct((nb, t_new, GROUP_W), F32)]
    scratch = [pltpu.VMEM((m, GROUP_W), F32), pltpu.VMEM((m, page), F32),
               pltpu.VMEM((PAGE_SLOTS, pps, GROUP_W, page), F32),
               pltpu.VMEM((PAGE_SLOTS, pps, GROUP_W, page), F32),
               pltpu.SemaphoreType.DMA((PAGE_SLOTS, 2, pps))]
    operands = [qbd, bias_tile, k_new, v_new, _neg_suffix_matrix(page), cache_k, cache_v]
    body = functools.partial(_sb_decode_body, pps=pps, t_new=t_new, n_pages=n_pages)
    if rec is not None:
        t_rec, nk, lanes = rec[0].shape
        nv = rec[5].shape[1]
        per_seq = t_rec // nb
        assert nk == HEAD_DIM // 2 and lanes == LANES and t_rec == per_seq * nb
        assert per_seq % steps_per_seq == 0
        steps = lambda rows: pl.BlockSpec((per_seq, rows, LANES), lambda b, jj, pt: (b, 0, 0))
        state = pl.BlockSpec((1, nv // 8, HEAD_DIM, 8, LANES), lambda b, jj, pt: (0, 0, 0, 0, 0))
        in_specs += [steps(nk)] * 5 + [steps(nv), state]
        out_specs += [steps(nv), state]
        out_shape += [jax.ShapeDtypeStruct((t_rec, nv, LANES), F32),
                      jax.ShapeDtypeStruct(rec[6].shape, F32)]
        scratch += [pltpu.VMEM((nv // 8, HEAD_DIM, 8, LANES), F32),
                    pltpu.VMEM((5, per_seq, HEAD_DIM, LANES), F32)]
        operands += list(rec)
        body = functools.partial(_decode_rec_body, pps=pps, t_new=t_new, n_pages=n_pages,
                                 rec_steps=per_seq // steps_per_seq)
    grid_spec = pltpu.PrefetchScalarGridSpec(
        num_scalar_prefetch=1, grid=(nb, steps_per_seq),
        in_specs=in_specs, out_specs=out_specs, scratch_shapes=scratch)
    outs = pl.pallas_call(
        body,
        grid_spec=grid_spec,
        out_shape=out_shape,
        compiler_params=_params(("arbitrary", "arbitrary")),
        name="sb_decode",
    )(page_table.reshape(-1), *operands)
    return outs if rec is not None else outs[0]


def _rwkv_prep_body(pr_ref, sh_ref, mu_ref, w0_ref, a0_ref, w2_ref, a2_ref, g2_ref,
                    kk_ref, ka_ref, rk_ref, gs_ref,
                    w_o, nkk_o, b_o, km_o, r_o, v_o, g_o, bonus_o, prev_ref, *, seq_len):
    pr = pr_ref[0]
    tc = pr.shape[0]
    row = lax.broadcasted_iota(jnp.int32, pr.shape, 0)
    rolled = pltpu.roll(pr, 1, axis=0)
    if seq_len is None:
        @pl.when(pl.program_id(1) == 0)
        def _():
            prev_ref[...] = sh_ref[0]

        prev = jnp.where(row == 0, prev_ref[...], rolled)
        prev_ref[...] = pr[tc - 1:tc, :]
    else:
        prev = jnp.where(row % seq_len == 0, sh_ref[0], rolled)
    xs = pr + (prev - pr) * mu_ref[...]
    r = xs[:, 0:GROUP_W]
    k = xs[:, GROUP_W:2 * GROUP_W]
    v = xs[:, 2 * GROUP_W:3 * GROUP_W]
    wa = xs[:, 3 * GROUP_W:3 * GROUP_W + LANES]
    gd = xs[:, 3 * GROUP_W + LANES:]
    gs = gs_ref[...]
    w_raw = -_softplus(-(w0_ref[...] + _dot(jnp.tanh(wa).astype(BF16), w2_ref[...]))) - 0.5
    w_o[0] = jnp.exp(-jnp.exp(w_raw))
    a = _sigmoid(a0_ref[...] + _dot(wa.astype(BF16), a2_ref[...]))
    g_o[0] = _dot(_sigmoid(gd).astype(BF16), g2_ref[...])
    kk = k * kk_ref[...]
    ss = _dot((kk * kk).astype(BF16), gs)
    kk = kk / jnp.maximum(jnp.sqrt(ss), 1e-12)
    nkk_o[0] = -kk
    b_o[0] = kk * a
    km = k * (1.0 + (a - 1.0) * ka_ref[...])
    km_o[0] = km
    r_o[0] = r
    v_o[0] = v
    bonus_o[0] = _dot((r * km * rk_ref[...]).astype(BF16), gs) * v


def _rwkv_prep(pr, shift, seq_len, mu, w0, a0, w2p, a2p, g2b, k_k, k_a, r_k, gsum, tc):
    b, s, pw = pr.shape
    blk = pl.BlockSpec((1, tc, GROUP_W), lambda bi, c: (bi, c, 0))
    vec = _const_spec((1, GROUP_W))
    if seq_len is None:
        shift_spec = pl.BlockSpec((1, 1, pw), lambda bi, c: (bi, 0, 0))
    else:
        assert tc % seq_len == 0
        shift_spec = pl.BlockSpec((1, tc, pw), lambda bi, c: (bi, c, 0))
    return pl.pallas_call(
        functools.partial(_rwkv_prep_body, seq_len=seq_len),
        grid=(b, s // tc),
        in_specs=[pl.BlockSpec((1, tc, pw), lambda bi, c: (bi, c, 0)),
                  shift_spec,
                  _const_spec((1, pw)), vec, vec,
                  _const_spec((LANES, GROUP_W)), _const_spec((LANES, GROUP_W)),
                  _const_spec((LANES, GROUP_W)), vec, vec, vec,
                  _const_spec((GROUP_W, GROUP_W))],
        out_specs=[blk] * 8,
        out_shape=[jax.ShapeDtypeStruct((b, s, GROUP_W), F32)] * 8,
        scratch_shapes=[pltpu.VMEM((1, pw), F32)],
        compiler_params=_params(("parallel", "arbitrary")),
        name="rwkv_prep",
    )(pr, shift, mu, w0, a0, w2p, a2p, g2b, k_k, k_a, r_k, gsum)


def _spread_keys(x):
    low = lax.broadcasted_iota(jnp.int32, x.shape, 1) < LANES // 2
    other = pltpu.roll(x, LANES // 2, axis=1)
    return jnp.concatenate([jnp.where(low, x, other), jnp.where(low, other, x)], axis=0)


def _sum4(terms):
    parts = list(terms[:4])
    for i, x in enumerate(terms[4:]):
        parts[i % 4] = parts[i % 4] + x
    return (parts[0] + parts[1]) + (parts[2] + parts[3])


def _rec_step(keyed, t, v_ref, o_ref, s_ref):
    w_ref, nkk_ref, b_ref, km_ref, r_ref = keyed
    nkeys = s_ref.shape[1]

    def key_row(ref, k):
        return ref[t, pl.ds(k, 1), :]

    for blk in range(s_ref.shape[0]):
        rows = pl.ds(blk * 8, 8)
        sa = _sum4([s_ref[blk, k] * key_row(nkk_ref, k) for k in range(nkeys)])
        v = v_ref[t, rows, :]
        outs = []
        for k in range(nkeys):
            s = s_ref[blk, k] * key_row(w_ref, k) + sa * key_row(b_ref, k) + v * key_row(km_ref, k)
            s_ref[blk, k] = s
            outs.append(s * key_row(r_ref, k))
        o_ref[t, rows, :] = _sum4(outs)


def _rwkv_rec_body(w_ref, nkk_ref, b_ref, km_ref, r_ref, v_ref, s0_ref, o_ref, st_ref, s_ref,
                   *spread_ref, tc):
    c = pl.program_id(1)

    @pl.when(c == 0)
    def _():
        s_ref[...] = s0_ref[0]

    keyed = (w_ref, nkk_ref, b_ref, km_ref, r_ref)
    if spread_ref:
        def spread(i, carry):
            for dt in range(SPREAD_UNROLL):
                t = i * SPREAD_UNROLL + dt
                for j, ref in enumerate(keyed):
                    spread_ref[0][j, t] = _spread_keys(ref[t])
            return carry

        lax.fori_loop(0, tc // SPREAD_UNROLL, spread, 0)
        keyed = tuple(spread_ref[0].at[j] for j in range(len(keyed)))

    def step(t, carry):
        _rec_step(keyed, t, v_ref, o_ref, s_ref)
        return carry

    lax.fori_loop(0, tc, step, 0)

    @pl.when(c == pl.num_programs(1) - 1)
    def _():
        st_ref[0] = s_ref[...]


def _rwkv_rec(w, nkk, b, km, r, v, s0, tc):
    t, nk, l = w.shape
    nv = v.shape[1]
    kvec = pl.BlockSpec((tc, nk, LANES), lambda g, c: (c, 0, g))
    vvec = pl.BlockSpec((tc, nv, LANES), lambda g, c: (c, 0, g))
    st = pl.BlockSpec((1, nv // 8, HEAD_DIM, 8, LANES), lambda g, c: (g, 0, 0, 0, 0))
    scratch = [pltpu.VMEM((nv // 8, HEAD_DIM, 8, LANES), F32)]
    if nk < HEAD_DIM:
        assert tc % SPREAD_UNROLL == 0
        scratch.append(pltpu.VMEM((5, tc, HEAD_DIM, LANES), F32))
    return pl.pallas_call(
        functools.partial(_rwkv_rec_body, tc=tc),
        grid=(l // LANES, t // tc),
        in_specs=[kvec] * 5 + [vvec, st],
        out_specs=[vvec, st],
        out_shape=[jax.ShapeDtypeStruct((t, nv, l), F32),
                   jax.ShapeDtypeStruct(s0.shape, F32)],
        scratch_shapes=scratch,
        compiler_params=_params(("parallel", "arbitrary")),
        name="rwkv_rec",
    )(w, nkk, b, km, r, v, s0)


def _decode_rec_body(pt_ref, qbd_ref, bias_ref, knew_ref, vnew_ref, u_ref, ck_hbm, cv_hbm,
                     w_ref, nkk_ref, b_ref, km_ref, r_ref, v_ref, s0_ref,
                     o_ref, orec_ref, st_ref,
                     acc_ref, carry_ref, kbuf, vbuf, sem, s_ref, spread_ref, *, rec_steps,
                     **decode_args):
    jj = pl.program_id(1)
    step = pl.program_id(0) * pl.num_programs(1) + jj

    @pl.when(step == 0)
    def _():
        s_ref[...] = s0_ref[0]

    @pl.when(jj == 0)
    def _():
        for j, ref in enumerate((w_ref, nkk_ref, b_ref, km_ref, r_ref)):
            for t in range(ref.shape[0]):
                spread_ref[j, t] = _spread_keys(ref[t])

    def recurrence():
        keyed = tuple(spread_ref.at[j] for j in range(5))
        for dt in range(rec_steps):
            _rec_step(keyed, jj * rec_steps + dt, v_ref, orec_ref, s_ref)

    _sb_decode_body(pt_ref, qbd_ref, bias_ref, knew_ref, vnew_ref, u_ref, ck_hbm, cv_hbm, o_ref,
                    acc_ref, carry_ref, kbuf, vbuf, sem, side_work=recurrence, **decode_args)

    @pl.when(step == pl.num_programs(0) * pl.num_programs(1) - 1)
    def _():
        st_ref[0] = s_ref[...]


def _out_ffn_body(x_ref, att_ref, o_ref, g_ref, bonus_ref, gnw_ref, gnb_ref, gm_ref,
                  wo_ref, n2_ref, wg_ref, wu_ref, wd_ref, y_ref):
    gm = gm_ref[...]
    o = o_ref[...]
    mean = _dot(o.astype(BF16), gm)
    d = o - mean
    var = _dot((d * d).astype(BF16), gm)
    rw = (d * lax.rsqrt(var + GN_EPS) * gnw_ref[...] + gnb_ref[...] + bonus_ref[...]) * g_ref[...]
    h = (x_ref[...] + _dot(att_ref[...].astype(BF16), wo_ref[0:GROUP_W, :])
         + _dot(rw.astype(BF16), wo_ref[GROUP_W:, :]))
    hn = h * lax.rsqrt(jnp.mean(h * h, axis=-1, keepdims=True) + NORM_EPS) * n2_ref[...]
    hb = hn.astype(BF16)
    gate = _dot(hb, wg_ref[...])
    act = gate * _sigmoid(gate) * _dot(hb, wu_ref[...])
    y_ref[...] = h + _dot(act.astype(BF16), wd_ref[...])


def _out_ffn(x, att, o, g, bonus, gn_w, gn_b, gmean, wo_b, n2, wg_b, wu_b, wd_b, tm):
    n, d = x.shape
    dff = wg_b.shape[1]
    tok = lambda w: pl.BlockSpec((tm, w), lambda i: (i, 0))
    once = lambda shape: pl.BlockSpec(shape, lambda i: (0, 0), pipeline_mode=pl.Buffered(1))
    return pl.pallas_call(
        _out_ffn_body,
        grid=(n // tm,),
        in_specs=[tok(d), tok(GROUP_W), tok(GROUP_W), tok(GROUP_W), tok(GROUP_W),
                  once((1, GROUP_W)), once((1, GROUP_W)), once((GROUP_W, GROUP_W)),
                  once((2 * GROUP_W, d)), once((1, d)),
                  once((d, dff)), once((d, dff)), once((dff, d))],
        out_specs=tok(d),
        out_shape=jax.ShapeDtypeStruct((n, d), F32),
        compiler_params=_params(("parallel",)),
        name="out_ffn",
    )(x, att, o, g, bonus, gn_w, gn_b, gmean, wo_b, n2, wg_b, wu_b, wd_b)


def _to_chain_lanes(x, halves):
    b, t, _ = x.shape
    if halves:
        y = x.reshape(b, t, N_HEADS, 2, HEAD_DIM // 2).transpose(1, 4, 3, 0, 2)
        return y.reshape(t, HEAD_DIM // 2, 2 * b * N_HEADS)
    y = x.reshape(b, t, N_HEADS, HEAD_DIM).transpose(1, 3, 2, 0)
    return y.reshape(t, HEAD_DIM, N_HEADS * b)


def _from_chain_lanes(o, b, halves):
    t = o.shape[0]
    if halves:
        y = o.reshape(t, HEAD_DIM // 2, 2, b, N_HEADS).transpose(3, 0, 4, 2, 1)
    else:
        y = o.reshape(t, HEAD_DIM, N_HEADS, b).transpose(3, 0, 2, 1)
    return y.reshape(b, t, GROUP_W)


def _state_to_chain_lanes(s):
    b = s.shape[0]
    y = s.transpose(1, 2, 3, 0)
    if b % LANES == 0:
        y = y.reshape(N_HEADS, HEAD_DIM, HEAD_DIM, b // LANES, LANES).transpose(0, 3, 1, 2, 4)
    else:
        y = y.transpose(1, 2, 0, 3)
    y = y.reshape(N_HEADS * b // LANES, HEAD_DIM // 8, 8, HEAD_DIM, LANES)
    return y.transpose(0, 1, 3, 2, 4)


def _state_from_chain_lanes(s, b, halves):
    s = s.transpose(0, 1, 3, 2, 4)
    if halves:
        y = s.reshape(HEAD_DIM // 2, HEAD_DIM, 2, b, N_HEADS).transpose(3, 4, 2, 0, 1)
    elif b % LANES == 0:
        y = s.reshape(N_HEADS, b // LANES, HEAD_DIM, HEAD_DIM, LANES).transpose(1, 4, 0, 2, 3)
    else:
        y = s.reshape(HEAD_DIM, HEAD_DIM, N_HEADS, b).transpose(3, 2, 0, 1)
    return y.reshape(b, N_HEADS, HEAD_DIM, HEAD_DIM)


def _rwkv_operands(pr, shift_prev, s0, prm, tc_prep):
    b, t, pw = pr.shape
    weights = (prm["mu"], prm["w0"], prm["a0"], prm["w2p"], prm["a2p"], prm["g2b"],
               prm["k_k"], prm["k_a"], prm["r_k"], prm["gsum"])
    if t >= tc_prep:
        vecs = _rwkv_prep(pr, shift_prev[:, None, :], None, *weights, tc_prep)
    else:
        starts = jnp.zeros((b, t, pw), F32).at[:, 0, :].set(shift_prev)
        vecs = _rwkv_prep(pr.reshape(1, b * t, pw), starts.reshape(1, b * t, pw), t,
                          *weights, min(tc_prep, b * t))
        vecs = [a.reshape(b, t, GROUP_W) for a in vecs]
    w, nkk, bv, km, r, v, g, bonus = vecs
    halves = 2 * b * N_HEADS == LANES
    vecs = [_to_chain_lanes(a, halves) for a in (w, nkk, bv, km, r, v)]
    if halves:
        assert s0 is None
        s0c = jnp.zeros((1, HEAD_DIM // 16, HEAD_DIM, 8, LANES), F32)
    else:
        s0c = _state_to_chain_lanes(s0)
    return (*vecs, s0c), g, bonus, halves


def _group_matrix(scale):
    i = jnp.arange(GROUP_W) // HEAD_DIM
    return jnp.where(i[:, None] == i[None, :], scale, 0.0).astype(BF16)


def kernel(x_prompt, x_sample, cache_k, cache_v, state_rwkv, state_shift, page_table, w_in, q_norm, k_norm, sb_bias, mu_shift, w0, w2, a0, a2, g2, k_k, k_a, r_k, gn_w, gn_b, w_o, norm1, norm2, w_gate, w_up, w_down):
    depth = w_in.shape[0]
    bp, sp, d = x_prompt.shape
    bs, ts, _ = x_sample.shape
    gmean = _group_matrix(1.0 / HEAD_DIM)
    gsum = _group_matrix(1.0)
    yp = x_prompt.reshape(bp * sp, d)
    ys = x_sample.reshape(bs * ts, d)
    outs = [[] for _ in range(8)]
    for l in range(depth):
        row = lambda a: a[l][None, :]
        w_in_b = w_in[l].astype(BF16)
        qg = jnp.tile(q_norm[l], N_HEADS)[None, :]
        kg = jnp.tile(k_norm[l], N_HEADS)[None, :]
        lora_pad = jnp.zeros((LANES // 2, GROUP_W), F32)
        prm = dict(mu=row(mu_shift), w0=row(w0), a0=row(a0),
                   w2p=jnp.concatenate([w2[l], lora_pad], axis=0).astype(BF16),
                   a2p=jnp.concatenate([lora_pad, a2[l]], axis=0).astype(BF16),
                   g2b=g2[l].astype(BF16), k_k=row(k_k), k_a=row(k_a), r_k=row(r_k), gsum=gsum)
        ffn = (row(gn_w), row(gn_b), gmean, w_o[l].astype(BF16), row(norm2),
               w_gate[l].astype(BF16), w_up[l].astype(BF16), w_down[l].astype(BF16))

        tm_p = min(TOKEN_TILE, bp * sp)
        tm_s = min(TOKEN_TILE, bs * ts)
        q_p, k_p, kb_p, v_p, vb_p, pr_p = _inproj(yp, row(norm1), w_in_b, qg, kg, gmean, tm_p)
        q_s, k_s, _, v_s, _, pr_s = _inproj(ys, row(norm1), w_in_b, qg, kg, gmean, tm_s)
        pr_p = pr_p.reshape(bp, sp, RWKV_PROJ_W)
        pr_s = pr_s.reshape(bs, ts, RWKV_PROJ_W)
        seq_p = lambda a: a.reshape(bp, sp, GROUP_W)
        seq_s = lambda a: a.reshape(bs, ts, GROUP_W)

        att_p = _sb_prompt(seq_p(q_p), seq_p(kb_p), seq_p(vb_p), sb_bias[l], min(ATT_TILE, sp))
        rec_p, g_p, bonus_p, halves_p = _rwkv_operands(
            pr_p, jnp.zeros((bp, RWKV_PROJ_W), F32), None, prm, PREP_CHUNK)
        rec_s, g_s, bonus_s, halves_s = _rwkv_operands(pr_s, state_shift[l], state_rwkv[l], prm,
                                                       PREP_CHUNK)
        n_pool, page = cache_k.shape[1], cache_k.shape[2]
        pool = lambda c: c[l].transpose(0, 2, 3, 1).reshape(n_pool, GROUP_W, page)
        pps = min(PAGES_PER_STEP, page_table.shape[1])
        decode = functools.partial(_sb_decode, seq_s(q_s), seq_s(k_s), seq_s(v_s), sb_bias[l],
                                   pool(cache_k), pool(cache_v), page_table, pps)
        decode_steps = bs * (page_table.shape[1] // pps)
        if halves_p and sp % decode_steps == 0:
            att_s, o_p, last_p = decode(rec=rec_p)
        else:
            att_s = decode()
            o_p, last_p = _rwkv_rec(*rec_p, min(REC_CHUNK, sp))
        o_s, last_s = _rwkv_rec(*rec_s, min(REC_CHUNK, ts))

        yp = _out_ffn(yp, seq_p(att_p).reshape(bp * sp, GROUP_W),
                      _from_chain_lanes(o_p, bp, halves_p).reshape(bp * sp, GROUP_W),
                      g_p.reshape(bp * sp, GROUP_W), bonus_p.reshape(bp * sp, GROUP_W), *ffn, tm_p)
        ys = _out_ffn(ys, att_s.reshape(bs * ts, GROUP_W),
                      _from_chain_lanes(o_s, bs, halves_s).reshape(bs * ts, GROUP_W),
                      g_s.reshape(bs * ts, GROUP_W), bonus_s.reshape(bs * ts, GROUP_W), *ffn, tm_s)
        outs[0].append(k_p.reshape(bp, sp, N_HEADS, HEAD_DIM))
        outs[1].append(v_p.reshape(bp, sp, N_HEADS, HEAD_DIM))
        outs[2].append(_state_from_chain_lanes(last_p, bp, halves_p))
        outs[3].append(pr_p[:, -1])
        outs[4].append(k_s.reshape(bs, ts, N_HEADS, HEAD_DIM))
        outs[5].append(v_s.reshape(bs, ts, N_HEADS, HEAD_DIM))
        outs[6].append(_state_from_chain_lanes(last_s, bs, halves_s))
        outs[7].append(pr_s[:, -1])
    stacked = [jnp.stack(o) for o in outs]
    return (yp.reshape(bp, sp, d), ys.reshape(bs, ts, d), *stacked)
```

```python
import functools

import jax
import jax.numpy as jnp
from jax import lax
from jax.experimental import pallas as pl
from jax.experimental.pallas import tpu as pltpu

F32 = jnp.float32
BF16 = jnp.bfloat16

HEAD_DIM = 64
N_HEADS = 8
GROUP_W = N_HEADS * HEAD_DIM
LORA_W = 256
RWKV_PROJ_W = 3 * GROUP_W + LORA_W
SB_SCALE = HEAD_DIM ** -0.5
LOG2E = 1.4426950408889634
NORM_EPS = 1e-6
GN_EPS = 64e-5
LANES = 128
VMEM_LIMIT = 56 * 1024 * 1024
TOKEN_TILE = 256
ATT_TILE = 256
PREP_CHUNK = 256
REC_CHUNK = 32
SPREAD_UNROLL = 8
PAGES_PER_STEP = 8
PAGE_SLOTS = 3


def _dot(a, b):
    return jnp.dot(a, b, preferred_element_type=F32)


def _dot_nt(a, b):
    return lax.dot_general(a, b, (((1,), (1,)), ((), ())), preferred_element_type=F32)


def _softplus(y):
    return jnp.maximum(y, 0.0) + jnp.log(1.0 + jnp.exp2(jnp.abs(y) * -LOG2E))


def _sigmoid(y):
    return 1.0 / (1.0 + jnp.exp(-y))


def _params(sem):
    return pltpu.CompilerParams(dimension_semantics=sem, vmem_limit_bytes=VMEM_LIMIT)


def _const_spec(shape):
    nd = len(shape)
    return pl.BlockSpec(shape, lambda *_: (0,) * nd)


def _inproj_body(x_ref, n1_ref, w_ref, qg_ref, kg_ref, gm_ref,
                 q_ref, k_ref, kb_ref, v_ref, vb_ref, pr_ref):
    x = x_ref[...]
    xn = x * lax.rsqrt(jnp.mean(x * x, axis=-1, keepdims=True) + NORM_EPS) * n1_ref[...]
    xb = xn.astype(BF16)

    def head_norm(t, g):
        ms = _dot((t * t).astype(BF16), gm_ref[...])
        return t * lax.rsqrt(ms + NORM_EPS) * g

    q = _dot(xb, w_ref[:, 0:GROUP_W])
    q_ref[...] = (head_norm(q, qg_ref[...]) * SB_SCALE).astype(BF16)
    k = head_norm(_dot(xb, w_ref[:, GROUP_W:2 * GROUP_W]), kg_ref[...])
    kb_ref[...] = k.astype(BF16)
    v = _dot(xb, w_ref[:, 2 * GROUP_W:3 * GROUP_W])
    vb_ref[...] = v.astype(BF16)
    for h in range(N_HEADS):
        k_ref[:, h, :] = k[:, h * HEAD_DIM:(h + 1) * HEAD_DIM]
        v_ref[:, h, :] = v[:, h * HEAD_DIM:(h + 1) * HEAD_DIM]
    pr_ref[...] = _dot(xb, w_ref[:, 3 * GROUP_W:])


def _inproj(x, n1, w_in_b, qg, kg, gmean, tm):
    n, d = x.shape
    pw = w_in_b.shape[1]
    tok = lambda w: pl.BlockSpec((tm, w), lambda i: (i, 0))
    heads = pl.BlockSpec((tm, N_HEADS, HEAD_DIM), lambda i: (i, 0, 0))
    return pl.pallas_call(
        _inproj_body,
        grid=(n // tm,),
        in_specs=[tok(d), _const_spec((1, d)), _const_spec((d, pw)),
                  _const_spec((1, GROUP_W)), _const_spec((1, GROUP_W)),
                  _const_spec((GROUP_W, GROUP_W))],
        out_specs=[tok(GROUP_W), heads, tok(GROUP_W), heads, tok(GROUP_W),
                   tok(RWKV_PROJ_W)],
        out_shape=[jax.ShapeDtypeStruct((n, GROUP_W), BF16),
                   jax.ShapeDtypeStruct((n, N_HEADS, HEAD_DIM), F32),
                   jax.ShapeDtypeStruct((n, GROUP_W), BF16),
                   jax.ShapeDtypeStruct((n, N_HEADS, HEAD_DIM), F32),
                   jax.ShapeDtypeStruct((n, GROUP_W), BF16),
                   jax.ShapeDtypeStruct((n, RWKV_PROJ_W), F32)],
        compiler_params=_params(("parallel",)),
        name="inproj",
    )(x, n1, w_in_b, qg, kg, gmean)


def _sb_weights(z, neg_u, carry, mask):
    sp = _softplus(z)
    if mask is not None:
        sp = jnp.where(mask, sp, 0.0)
    suffix = _dot(sp.astype(BF16), neg_u)
    weight = jnp.exp2(((z - sp) + suffix + carry) * LOG2E)
    if mask is not None:
        weight = jnp.where(mask, weight, 0.0)
    return weight.astype(BF16), carry - jnp.sum(sp, axis=1, keepdims=True)


def _neg_suffix_matrix(tk):
    j = lax.broadcasted_iota(jnp.int32, (tk, tk), 0)
    s = lax.broadcasted_iota(jnp.int32, (tk, tk), 1)
    return jnp.where(j > s, -1.0, 0.0).astype(BF16)


def _sb_prompt_body(bias_ref, q_ref, k_ref, v_ref, u_ref, o_ref, *, tq):
    hp = pl.program_id(1)
    qi = pl.program_id(2)
    u = u_ref[...]
    q = q_ref[0]
    lane_head = lax.broadcasted_iota(jnp.int32, (tq, LANES), 1) // HEAD_DIM
    zero = jnp.zeros_like(q)
    qs = jnp.concatenate([jnp.where(lane_head == 0, q, zero),
                          jnp.where(lane_head == 1, q, zero)], axis=0)
    bias0 = bias_ref[hp * 2]
    bias1 = bias_ref[hp * 2 + 1]
    qpos = lax.broadcasted_iota(jnp.int32, (tq, tq), 0)
    kpos = lax.broadcasted_iota(jnp.int32, (tq, tq), 1)
    causal = jnp.concatenate([kpos < qpos] * 2, axis=0)

    def logits(kb):
        start = pl.multiple_of(kb * tq, tq)
        zz = _dot_nt(qs, k_ref[0, pl.ds(start, tq), :])
        return jnp.concatenate([zz[:tq] + bias0, zz[tq:] + bias1], axis=0)

    def values(kb, w):
        start = pl.multiple_of(kb * tq, tq)
        return _dot(w, v_ref[0, pl.ds(start, tq), :])

    w, carry = _sb_weights(logits(qi), u, jnp.zeros((2 * tq, 1), F32), causal)

    def body(i, state):
        z, w_newer, carry, acc = state
        kb = qi - 1 - i
        z_older = logits(jnp.maximum(kb - 1, 0))
        out_newer = values(kb + 1, w_newer)
        w, carry = _sb_weights(z, u, carry, None)
        return z_older, w, carry, acc + out_newer

    state = (logits(jnp.maximum(qi - 1, 0)), w, carry, jnp.zeros((2 * tq, LANES), F32))
    _, w, _, acc = lax.fori_loop(0, qi, body, state)
    acc = acc + values(0, w)
    o_ref[0] = jnp.where(lane_head == 0, acc[:tq], acc[tq:])


def _sb_prompt(q_b, k_b, v_b, bias, tq):
    b, s, _ = q_b.shape
    hpairs = GROUP_W // LANES
    return pl.pallas_call(
        functools.partial(_sb_prompt_body, tq=tq),
        grid=(b, hpairs, s // tq),
        in_specs=[pl.BlockSpec(memory_space=pltpu.SMEM),
                  pl.BlockSpec((1, tq, LANES), lambda bi, hp, qi: (bi, qi, hp)),
                  pl.BlockSpec((1, s, LANES), lambda bi, hp, qi: (bi, 0, hp)),
                  pl.BlockSpec((1, s, LANES), lambda bi, hp, qi: (bi, 0, hp)),
                  _const_spec((tq, tq))],
        out_specs=pl.BlockSpec((1, tq, LANES), lambda bi, hp, qi: (bi, qi, hp)),
        out_shape=jax.ShapeDtypeStruct((b, s, GROUP_W), F32),
        compiler_params=_params(("parallel", "parallel", "arbitrary")),
        name="sb_prompt",
    )(bias, q_b, k_b, v_b, _neg_suffix_matrix(tq))


def _sb_decode_body(pt_ref, qbd_ref, bias_ref, knew_ref, vnew_ref, u_ref, ck_hbm, cv_hbm, o_ref,
                    acc_ref, carry_ref, kbuf, vbuf, sem, *, pps, t_new, n_pages, side_work=()):
    jj = pl.program_id(1)
    steps_per_seq = pl.num_programs(1)
    step = pl.program_id(0) * steps_per_seq + jj
    n_steps = pl.num_programs(0) * steps_per_seq
    qbd = qbd_ref[0]
    bias = bias_ref[...]
    u = u_ref[...]
    m, page = bias.shape

    def page_copies(s):
        slot = s % PAGE_SLOTS
        first = (s // steps_per_seq) * n_pages + (n_pages - 1) - (s % steps_per_seq) * pps
        copies = []
        for i in range(pps):
            pid = pt_ref[first - i]
            copies.append(pltpu.make_async_copy(ck_hbm.at[pid], kbuf.at[slot, i], sem.at[slot, 0, i]))
            copies.append(pltpu.make_async_copy(cv_hbm.at[pid], vbuf.at[slot, i], sem.at[slot, 1, i]))
        return copies

    @pl.when(step == 0)
    def _():
        for s in range(PAGE_SLOTS - 1):
            @pl.when(s < n_steps)
            def _(s=s):
                for c in page_copies(s):
                    c.start()

    @pl.when(step + (PAGE_SLOTS - 1) < n_steps)
    def _():
        for c in page_copies(step + (PAGE_SLOTS - 1)):
            c.start()

    for c in page_copies(step):
        c.wait()
    slot = step % PAGE_SLOTS
    k_refs = [kbuf.at[slot, i] for i in range(pps)]
    v_refs = [vbuf.at[slot, i] for i in range(pps)]

    @pl.when(jj == 0)
    def _():
        pad = jnp.zeros((page - knew_ref.shape[1], GROUP_W), F32)
        kb = jnp.concatenate([knew_ref[0], pad], axis=0).astype(BF16)
        vb = jnp.concatenate([vnew_ref[0], pad], axis=0).astype(BF16)
        qpos = lax.broadcasted_iota(jnp.int32, (m, page), 0) // N_HEADS
        kpos = lax.broadcasted_iota(jnp.int32, (m, page), 1)
        w, carry = _sb_weights(_dot_nt(qbd, kb) + bias, u, jnp.zeros((m, page), F32), kpos < qpos)
        acc_ref[...] = _dot(w, vb)
        carry_ref[...] = carry

    z = jnp.concatenate([_dot(qbd, k_refs[i][...].astype(BF16)) + bias for i in range(pps)], axis=0)
    sp = _softplus(z)
    suffix = _dot(sp.astype(BF16), u)
    sp_sum = jnp.sum(sp, axis=1, keepdims=True)
    carry = carry_ref[...]
    carries = []
    for i in range(pps):
        carries.append(carry)
        carry = carry - sp_sum[i * m:(i + 1) * m]
    carry_ref[...] = carry
    weight = jnp.exp2(((z - sp) + suffix + jnp.concatenate(carries, axis=0)) * LOG2E).astype(BF16)
    acc = acc_ref[...]
    for i in range(pps):
        acc = acc + _dot_nt(weight[i * m:(i + 1) * m], v_refs[i][...].astype(BF16))
    acc_ref[...] = acc
    for piece in side_work:
        piece()

    @pl.when(jj == pl.num_programs(1) - 1)
    def _():
        row_head = lax.broadcasted_iota(jnp.int32, (m, GROUP_W), 0) % N_HEADS
        lane_head = lax.broadcasted_iota(jnp.int32, (m, GROUP_W), 1) // HEAD_DIM
        own = jnp.where(row_head == lane_head, acc, 0.0)
        o_ref[0] = jnp.sum(own.reshape(t_new, N_HEADS, GROUP_W), axis=1)


def _sb_decode(q_b, k_new, v_new, bias, cache_k, cache_v, page_table, pps, rec=None):
    nb, t_new, _ = q_b.shape
    n_pages = page_table.shape[1]
    page = cache_k.shape[2]
    m = t_new * N_HEADS
    t_pad = -(-t_new // 8) * 8
    lane_head = jnp.arange(GROUP_W) // HEAD_DIM
    sel = (lane_head[None, :] == jnp.arange(N_HEADS)[:, None])
    qbd = jnp.where(sel[None, None], q_b[:, :, None, :], jnp.zeros((), BF16)).reshape(nb, m, GROUP_W)
    bias_tile = jnp.broadcast_to(jnp.tile(bias, t_new)[:, None], (m, page)).astype(F32)
    padn = ((0, 0), (0, t_pad - t_new), (0, 0))
    k_new = jnp.pad(k_new, padn)
    v_new = jnp.pad(v_new, padn)

    seq = lambda r, w: pl.BlockSpec((1, r, w), lambda b, jj, pt: (b, 0, 0))
    steps_per_seq = n_pages // pps
    in_specs = [seq(m, GROUP_W),
                pl.BlockSpec((m, page), lambda b, jj, pt: (0, 0)),
                seq(t_pad, GROUP_W), seq(t_pad, GROUP_W),
                pl.BlockSpec((page, page), lambda b, jj, pt: (0, 0)),
                pl.BlockSpec(memory_space=pl.ANY), pl.BlockSpec(memory_space=pl.ANY)]
    out_specs = [seq(t_new, GROUP_W)]
    out_shape = [jax.ShapeDtypeStruct((nb, t_new, GROUP_W), F32)]
    scratch = [pltpu.VMEM((m, GROUP_W), F32), pltpu.VMEM((m, page), F32),
               pltpu.VMEM((PAGE_SLOTS, pps, GROUP_W, page), F32),
               pltpu.VMEM((PAGE_SLOTS, pps, GROUP_W, page), F32),
               pltpu.SemaphoreType.DMA((PAGE_SLOTS, 2, pps))]
    operands = [qbd, bias_tile, k_new, v_new, _neg_suffix_matrix(page), cache_k, cache_v]
    body = functools.partial(_sb_decode_body, pps=pps, t_new=t_new, n_pages=n_pages)
    if rec is not None:
        t_rec, nk, lanes = rec[0].shape
        nv = rec[5].shape[1]
        per_seq = t_rec // nb
        assert nk == HEAD_DIM // 2 and lanes == LANES and t_rec == per_seq * nb
        assert per_seq % steps_per_seq == 0
        steps = lambda rows: pl.BlockSpec((per_seq, rows, LANES), lambda b, jj, pt: (b, 0, 0))
        state = pl.BlockSpec((1, nv // 8, HEAD_DIM, 8, LANES), lambda b, jj, pt: (0, 0, 0, 0, 0))
        in_specs += [steps(nk)] * 5 + [steps(nv), state]
        out_specs += [steps(nv), state]
        out_shape += [jax.ShapeDtypeStruct((t_rec, nv, LANES), F32),
                      jax.ShapeDtypeStruct(rec[6].shape, F32)]
        scratch += [pltpu.VMEM((nv // 8, HEAD_DIM, 8, LANES), F32),
                    pltpu.VMEM((5, per_seq, HEAD_DIM, LANES), F32)]
        operands += list(rec)
        body = functools.partial(_decode_rec_body, pps=pps, t_new=t_new, n_pages=n_pages,
                                 rec_steps=per_seq // steps_per_seq)
    grid_spec = pltpu.PrefetchScalarGridSpec(
        num_scalar_prefetch=1, grid=(nb, steps_per_seq),
        in_specs=in_specs, out_specs=out_specs, scratch_shapes=scratch)
    outs = pl.pallas_call(
        body,
        grid_spec=grid_spec,
        out_shape=out_shape,
        compiler_params=_params(("arbitrary", "arbitrary")),
        name="sb_decode",
    )(page_table.reshape(-1), *operands)
    return outs if rec is not None else outs[0]


def _rwkv_prep_body(pr_ref, sh_ref, mu_ref, w0_ref, a0_ref, w2_ref, a2_ref, g2_ref,
                    kk_ref, ka_ref, rk_ref, gs_ref,
                    w_o, nkk_o, b_o, km_o, r_o, v_o, g_o, bonus_o, prev_ref, *, seq_len):
    pr = pr_ref[0]
    tc = pr.shape[0]
    row = lax.broadcasted_iota(jnp.int32, pr.shape, 0)
    rolled = pltpu.roll(pr, 1, axis=0)
    if seq_len is None:
        @pl.when(pl.program_id(1) == 0)
        def _():
            prev_ref[...] = sh_ref[0]

        prev = jnp.where(row == 0, prev_ref[...], rolled)
        prev_ref[...] = pr[tc - 1:tc, :]
    else:
        prev = jnp.where(row % seq_len == 0, sh_ref[0], rolled)
    xs = pr + (prev - pr) * mu_ref[...]
    r = xs[:, 0:GROUP_W]
    k = xs[:, GROUP_W:2 * GROUP_W]
    v = xs[:, 2 * GROUP_W:3 * GROUP_W]
    wa = xs[:, 3 * GROUP_W:3 * GROUP_W + LANES]
    gd = xs[:, 3 * GROUP_W + LANES:]
    gs = gs_ref[...]
    w_raw = -_softplus(-(w0_ref[...] + _dot(jnp.tanh(wa).astype(BF16), w2_ref[...]))) - 0.5
    w_o[0] = jnp.exp(-jnp.exp(w_raw))
    a = _sigmoid(a0_ref[...] + _dot(wa.astype(BF16), a2_ref[...]))
    g_o[0] = _dot(_sigmoid(gd).astype(BF16), g2_ref[...])
    kk = k * kk_ref[...]
    ss = _dot((kk * kk).astype(BF16), gs)
    kk = kk / jnp.maximum(jnp.sqrt(ss), 1e-12)
    nkk_o[0] = -kk
    b_o[0] = kk * a
    km = k * (1.0 + (a - 1.0) * ka_ref[...])
    km_o[0] = km
    r_o[0] = r
    v_o[0] = v
    bonus_o[0] = _dot((r * km * rk_ref[...]).astype(BF16), gs) * v


def _rwkv_prep(pr, shift, seq_len, mu, w0, a0, w2p, a2p, g2b, k_k, k_a, r_k, gsum, tc):
    b, s, pw = pr.shape
    blk = pl.BlockSpec((1, tc, GROUP_W), lambda bi, c: (bi, c, 0))
    vec = _const_spec((1, GROUP_W))
    if seq_len is None:
        shift_spec = pl.BlockSpec((1, 1, pw), lambda bi, c: (bi, 0, 0))
    else:
        assert tc % seq_len == 0
        shift_spec = pl.BlockSpec((1, tc, pw), lambda bi, c: (bi, c, 0))
    return pl.pallas_call(
        functools.partial(_rwkv_prep_body, seq_len=seq_len),
        grid=(b, s // tc),
        in_specs=[pl.BlockSpec((1, tc, pw), lambda bi, c: (bi, c, 0)),
                  shift_spec,
                  _const_spec((1, pw)), vec, vec,
                  _const_spec((LANES, GROUP_W)), _const_spec((LANES, GROUP_W)),
                  _const_spec((LANES, GROUP_W)), vec, vec, vec,
                  _const_spec((GROUP_W, GROUP_W))],
        out_specs=[blk] * 8,
        out_shape=[jax.ShapeDtypeStruct((b, s, GROUP_W), F32)] * 8,
        scratch_shapes=[pltpu.VMEM((1, pw), F32)],
        compiler_params=_params(("parallel", "arbitrary")),
        name="rwkv_prep",
    )(pr, shift, mu, w0, a0, w2p, a2p, g2b, k_k, k_a, r_k, gsum)


def _spread_keys(x):
    low = lax.broadcasted_iota(jnp.int32, x.shape, 1) < LANES // 2
    other = pltpu.roll(x, LANES // 2, axis=1)
    return jnp.concatenate([jnp.where(low, x, other), jnp.where(low, other, x)], axis=0)


def _rec_step(keyed, t, v_ref, o_ref, s_ref):
    w_ref, nkk_ref, b_ref, km_ref, r_ref = keyed
    nblk, nkeys = s_ref.shape[:2]
    blocks = range(nblk)

    def key_row(ref, k):
        return jnp.broadcast_to(ref[t, pl.ds(k, 1), :], (8, LANES))

    parts = [[None, None] for _ in blocks]
    for k in range(nkeys):
        nkk = key_row(nkk_ref, k)
        for blk in blocks:
            term = s_ref[blk, k] * nkk
            prev = parts[blk][k % 2]
            parts[blk][k % 2] = term if prev is None else prev + term
    sa = [p[0] + p[1] for p in parts]
    v = [v_ref[t, pl.ds(blk * 8, 8), :] for blk in blocks]
    parts = [[None, None] for _ in blocks]
    for k in range(nkeys):
        w, b, km, r = (key_row(ref, k) for ref in (w_ref, b_ref, km_ref, r_ref))
        for blk in blocks:
            s = s_ref[blk, k] * w + sa[blk] * b + v[blk] * km
            s_ref[blk, k] = s
            prev = parts[blk][k % 2]
            parts[blk][k % 2] = s * r if prev is None else prev + s * r
    for blk in blocks:
        o_ref[t, pl.ds(blk * 8, 8), :] = parts[blk][0] + parts[blk][1]


def _rwkv_rec_body(w_ref, nkk_ref, b_ref, km_ref, r_ref, v_ref, s0_ref, o_ref, st_ref, s_ref,
                   *spread_ref, tc):
    c = pl.program_id(1)

    @pl.when(c == 0)
    def _():
        s_ref[...] = s0_ref[0]

    keyed = (w_ref, nkk_ref, b_ref, km_ref, r_ref)
    if spread_ref:
        def spread(i, carry):
            for dt in range(SPREAD_UNROLL):
                t = i * SPREAD_UNROLL + dt
                for j, ref in enumerate(keyed):
                    spread_ref[0][j, t] = _spread_keys(ref[t])
            return carry

        lax.fori_loop(0, tc // SPREAD_UNROLL, spread, 0)
        keyed = tuple(spread_ref[0].at[j] for j in range(len(keyed)))

    def step(t, carry):
        _rec_step(keyed, t, v_ref, o_ref, s_ref)
        return carry

    lax.fori_loop(0, tc, step, 0)

    @pl.when(c == pl.num_programs(1) - 1)
    def _():
        st_ref[0] = s_ref[...]


def _rwkv_rec(w, nkk, b, km, r, v, s0, tc):
    t, nk, l = w.shape
    nv = v.shape[1]
    kvec = pl.BlockSpec((tc, nk, LANES), lambda g, c: (c, 0, g))
    vvec = pl.BlockSpec((tc, nv, LANES), lambda g, c: (c, 0, g))
    st = pl.BlockSpec((1, nv // 8, HEAD_DIM, 8, LANES), lambda g, c: (g, 0, 0, 0, 0))
    scratch = [pltpu.VMEM((nv // 8, HEAD_DIM, 8, LANES), F32)]
    if nk < HEAD_DIM:
        assert tc % SPREAD_UNROLL == 0
        scratch.append(pltpu.VMEM((5, tc, HEAD_DIM, LANES), F32))
    return pl.pallas_call(
        functools.partial(_rwkv_rec_body, tc=tc),
        grid=(l // LANES, t // tc),
        in_specs=[kvec] * 5 + [vvec, st],
        out_specs=[vvec, st],
        out_shape=[jax.ShapeDtypeStruct((t, nv, l), F32),
                   jax.ShapeDtypeStruct(s0.shape, F32)],
        scratch_shapes=scratch,
        compiler_params=_params(("parallel", "arbitrary")),
        name="rwkv_rec",
    )(w, nkk, b, km, r, v, s0)


def _decode_rec_body(pt_ref, qbd_ref, bias_ref, knew_ref, vnew_ref, u_ref, ck_hbm, cv_hbm,
                     w_ref, nkk_ref, b_ref, km_ref, r_ref, v_ref, s0_ref,
                     o_ref, orec_ref, st_ref,
                     acc_ref, carry_ref, kbuf, vbuf, sem, s_ref, spread_ref, *, rec_steps,
                     **decode_args):
    jj = pl.program_id(1)
    step = pl.program_id(0) * pl.num_programs(1) + jj

    @pl.when(step == 0)
    def _():
        s_ref[...] = s0_ref[0]

    @pl.when(jj == 0)
    def _():
        for j, ref in enumerate((w_ref, nkk_ref, b_ref, km_ref, r_ref)):
            for t in range(ref.shape[0]):
                spread_ref[j, t] = _spread_keys(ref[t])

    keyed = tuple(spread_ref.at[j] for j in range(5))
    rec = [functools.partial(_rec_step, keyed, jj * rec_steps + dt, v_ref, orec_ref, s_ref)
           for dt in range(rec_steps)]
    _sb_decode_body(pt_ref, qbd_ref, bias_ref, knew_ref, vnew_ref, u_ref, ck_hbm, cv_hbm, o_ref,
                    acc_ref, carry_ref, kbuf, vbuf, sem, side_work=rec, **decode_args)

    @pl.when(step == pl.num_programs(0) * pl.num_programs(1) - 1)
    def _():
        st_ref[0] = s_ref[...]


def _out_ffn_body(x_ref, att_ref, o_ref, g_ref, bonus_ref, gnw_ref, gnb_ref, gm_ref,
                  wo_ref, n2_ref, wg_ref, wu_ref, wd_ref, y_ref):
    gm = gm_ref[...]
    o = o_ref[...]
    mean = _dot(o.astype(BF16), gm)
    d = o - mean
    var = _dot((d * d).astype(BF16), gm)
    rw = (d * lax.rsqrt(var + GN_EPS) * gnw_ref[...] + gnb_ref[...] + bonus_ref[...]) * g_ref[...]
    h = (x_ref[...] + _dot(att_ref[...].astype(BF16), wo_ref[0:GROUP_W, :])
         + _dot(rw.astype(BF16), wo_ref[GROUP_W:, :]))
    hn = h * lax.rsqrt(jnp.mean(h * h, axis=-1, keepdims=True) + NORM_EPS) * n2_ref[...]
    hb = hn.astype(BF16)
    gate = _dot(hb, wg_ref[...])
    act = gate * _sigmoid(gate) * _dot(hb, wu_ref[...])
    y_ref[...] = h + _dot(act.astype(BF16), wd_ref[...])


def _out_ffn(x, att, o, g, bonus, gn_w, gn_b, gmean, wo_b, n2, wg_b, wu_b, wd_b, tm):
    n, d = x.shape
    dff = wg_b.shape[1]
    tok = lambda w: pl.BlockSpec((tm, w), lambda i: (i, 0))
    once = lambda shape: pl.BlockSpec(shape, lambda i: (0, 0), pipeline_mode=pl.Buffered(1))
    return pl.pallas_call(
        _out_ffn_body,
        grid=(n // tm,),
        in_specs=[tok(d), tok(GROUP_W), tok(GROUP_W), tok(GROUP_W), tok(GROUP_W),
                  once((1, GROUP_W)), once((1, GROUP_W)), once((GROUP_W, GROUP_W)),
                  once((2 * GROUP_W, d)), once((1, d)),
                  once((d, dff)), once((d, dff)), once((dff, d))],
        out_specs=tok(d),
        out_shape=jax.ShapeDtypeStruct((n, d), F32),
        compiler_params=_params(("parallel",)),
        name="out_ffn",
    )(x, att, o, g, bonus, gn_w, gn_b, gmean, wo_b, n2, wg_b, wu_b, wd_b)


def _to_chain_lanes(x, halves):
    b, t, _ = x.shape
    if halves:
        y = x.reshape(b, t, N_HEADS, 2, HEAD_DIM // 2).transpose(1, 4, 3, 0, 2)
        return y.reshape(t, HEAD_DIM // 2, 2 * b * N_HEADS)
    y = x.reshape(b, t, N_HEADS, HEAD_DIM).transpose(1, 3, 2, 0)
    return y.reshape(t, HEAD_DIM, N_HEADS * b)


def _from_chain_lanes(o, b, halves):
    t = o.shape[0]
    if halves:
        y = o.reshape(t, HEAD_DIM // 2, 2, b, N_HEADS).transpose(3, 0, 4, 2, 1)
    else:
        y = o.reshape(t, HEAD_DIM, N_HEADS, b).transpose(3, 0, 2, 1)
    return y.reshape(b, t, GROUP_W)


def _state_to_chain_lanes(s):
    b = s.shape[0]
    y = s.transpose(1, 2, 3, 0)
    if b % LANES == 0:
        y = y.reshape(N_HEADS, HEAD_DIM, HEAD_DIM, b // LANES, LANES).transpose(0, 3, 1, 2, 4)
    else:
        y = y.transpose(1, 2, 0, 3)
    y = y.reshape(N_HEADS * b // LANES, HEAD_DIM // 8, 8, HEAD_DIM, LANES)
    return y.transpose(0, 1, 3, 2, 4)


def _state_from_chain_lanes(s, b, halves):
    s = s.transpose(0, 1, 3, 2, 4)
    if halves:
        y = s.reshape(HEAD_DIM // 2, HEAD_DIM, 2, b, N_HEADS).transpose(3, 4, 2, 0, 1)
    elif b % LANES == 0:
        y = s.reshape(N_HEADS, b // LANES, HEAD_DIM, HEAD_DIM, LANES).transpose(1, 4, 0, 2, 3)
    else:
        y = s.reshape(HEAD_DIM, HEAD_DIM, N_HEADS, b).transpose(3, 2, 0, 1)
    return y.reshape(b, N_HEADS, HEAD_DIM, HEAD_DIM)


def _rwkv_operands(pr, shift_prev, s0, prm, tc_prep):
    b, t, pw = pr.shape
    weights = (prm["mu"], prm["w0"], prm["a0"], prm["w2p"], prm["a2p"], prm["g2b"],
               prm["k_k"], prm["k_a"], prm["r_k"], prm["gsum"])
    if t >= tc_prep:
        vecs = _rwkv_prep(pr, shift_prev[:, None, :], None, *weights, tc_prep)
    else:
        starts = jnp.zeros((b, t, pw), F32).at[:, 0, :].set(shift_prev)
        vecs = _rwkv_prep(pr.reshape(1, b * t, pw), starts.reshape(1, b * t, pw), t,
                          *weights, min(tc_prep, b * t))
        vecs = [a.reshape(b, t, GROUP_W) for a in vecs]
    w, nkk, bv, km, r, v, g, bonus = vecs
    halves = 2 * b * N_HEADS == LANES
    vecs = [_to_chain_lanes(a, halves) for a in (w, nkk, bv, km, r, v)]
    if halves:
        assert s0 is None
        s0c = jnp.zeros((1, HEAD_DIM // 16, HEAD_DIM, 8, LANES), F32)
    else:
        s0c = _state_to_chain_lanes(s0)
    return (*vecs, s0c), g, bonus, halves


def _group_matrix(scale):
    i = jnp.arange(GROUP_W) // HEAD_DIM
    return jnp.where(i[:, None] == i[None, :], scale, 0.0).astype(BF16)


def kernel(x_prompt, x_sample, cache_k, cache_v, state_rwkv, state_shift, page_table, w_in, q_norm, k_norm, sb_bias, mu_shift, w0, w2, a0, a2, g2, k_k, k_a, r_k, gn_w, gn_b, w_o, norm1, norm2, w_gate, w_up, w_down):
    depth = w_in.shape[0]
    bp, sp, d = x_prompt.shape
    bs, ts, _ = x_sample.shape
    gmean = _group_matrix(1.0 / HEAD_DIM)
    gsum = _group_matrix(1.0)
    yp = x_prompt.reshape(bp * sp, d)
    ys = x_sample.reshape(bs * ts, d)
    outs = [[] for _ in range(8)]
    for l in range(depth):
        row = lambda a: a[l][None, :]
        w_in_b = w_in[l].astype(BF16)
        qg = jnp.tile(q_norm[l], N_HEADS)[None, :]
        kg = jnp.tile(k_norm[l], N_HEADS)[None, :]
        lora_pad = jnp.zeros((LANES // 2, GROUP_W), F32)
        prm = dict(mu=row(mu_shift), w0=row(w0), a0=row(a0),
                   w2p=jnp.concatenate([w2[l], lora_pad], axis=0).astype(BF16),
                   a2p=jnp.concatenate([lora_pad, a2[l]], axis=0).astype(BF16),
                   g2b=g2[l].astype(BF16), k_k=row(k_k), k_a=row(k_a), r_k=row(r_k), gsum=gsum)
        ffn = (row(gn_w), row(gn_b), gmean, w_o[l].astype(BF16), row(norm2),
               w_gate[l].astype(BF16), w_up[l].astype(BF16), w_down[l].astype(BF16))

        tm_p = min(TOKEN_TILE, bp * sp)
        tm_s = min(TOKEN_TILE, bs * ts)
        q_p, k_p, kb_p, v_p, vb_p, pr_p = _inproj(yp, row(norm1), w_in_b, qg, kg, gmean, tm_p)
        q_s, k_s, _, v_s, _, pr_s = _inproj(ys, row(norm1), w_in_b, qg, kg, gmean, tm_s)
        pr_p = pr_p.reshape(bp, sp, RWKV_PROJ_W)
        pr_s = pr_s.reshape(bs, ts, RWKV_PROJ_W)
        seq_p = lambda a: a.reshape(bp, sp, GROUP_W)
        seq_s = lambda a: a.reshape(bs, ts, GROUP_W)

        att_p = _sb_prompt(seq_p(q_p), seq_p(kb_p), seq_p(vb_p), sb_bias[l], min(ATT_TILE, sp))
        rec_p, g_p, bonus_p, halves_p = _rwkv_operands(
            pr_p, jnp.zeros((bp, RWKV_PROJ_W), F32), None, prm, PREP_CHUNK)
        rec_s, g_s, bonus_s, halves_s = _rwkv_operands(pr_s, state_shift[l], state_rwkv[l], prm,
                                                       PREP_CHUNK)
        n_pool, page = cache_k.shape[1], cache_k.shape[2]
        pool = lambda c: c[l].transpose(0, 2, 3, 1).reshape(n_pool, GROUP_W, page)
        pps = min(PAGES_PER_STEP, page_table.shape[1])
        decode = functools.partial(_sb_decode, seq_s(q_s), seq_s(k_s), seq_s(v_s), sb_bias[l],
                                   pool(cache_k), pool(cache_v), page_table, pps)
        decode_steps = bs * (page_table.shape[1] // pps)
        if halves_p and sp % decode_steps == 0:
            att_s, o_p, last_p = decode(rec=rec_p)
        else:
            att_s = decode()
            o_p, last_p = _rwkv_rec(*rec_p, min(REC_CHUNK, sp))
        o_s, last_s = _rwkv_rec(*rec_s, min(REC_CHUNK, ts))

        yp = _out_ffn(yp, seq_p(att_p).reshape(bp * sp, GROUP_W),
                      _from_chain_lanes(o_p, bp, halves_p).reshape(bp * sp, GROUP_W),
                      g_p.reshape(bp * sp, GROUP_W), bonus_p.reshape(bp * sp, GROUP_W), *ffn, tm_p)
        ys = _out_ffn(ys, att_s.reshape(bs * ts, GROUP_W),
                      _from_chain_lanes(o_s, bs, halves_s).reshape(bs * ts, GROUP_W),
                      g_s.reshape(bs * ts, GROUP_W), bonus_s.reshape(bs * ts, GROUP_W), *ffn, tm_s)
        outs[0].append(k_p.reshape(bp, sp, N_HEADS, HEAD_DIM))
        outs[1].append(v_p.reshape(bp, sp, N_HEADS, HEAD_DIM))
        outs[2].append(_state_from_chain_lanes(last_p, bp, halves_p))
        outs[3].append(pr_p[:, -1])
        outs[4].append(k_s.reshape(bs, ts, N_HEADS, HEAD_DIM))
        outs[5].append(v_s.reshape(bs, ts, N_HEADS, HEAD_DIM))
        outs[6].append(_state_from_chain_lanes(last_s, bs, halves_s))
        outs[7].append(pr_s[:, -1])
    stacked = [jnp.stack(o) for o in outs]
    return (yp.reshape(bp, sp, d), ys.reshape(bs, ts, d), *stacked)
```

```python
import functools

import jax
import jax.numpy as jnp
from jax import lax
from jax.experimental import pallas as pl
from jax.experimental.pallas import tpu as pltpu

F32 = jnp.float32
BF16 = jnp.bfloat16

HEAD_DIM = 64
N_HEADS = 8
GROUP_W = N_HEADS * HEAD_DIM
LORA_W = 256
RWKV_PROJ_W = 3 * GROUP_W + LORA_W
SB_SCALE = HEAD_DIM ** -0.5
LOG2E = 1.4426950408889634
NORM_EPS = 1e-6
GN_EPS = 64e-5
LANES = 128
VMEM_LIMIT = 56 * 1024 * 1024
TOKEN_TILE = 256
ATT_TILE = 256
PREP_CHUNK = 256
REC_CHUNK = 32
SPREAD_UNROLL = 8
PAGES_PER_STEP = 8
PAGE_SLOTS = 3


def _dot(a, b):
    return jnp.dot(a, b, preferred_element_type=F32)


def _dot_nt(a, b):
    return lax.dot_general(a, b, (((1,), (1,)), ((), ())), preferred_element_type=F32)


def _softplus(y):
    return jnp.maximum(y, 0.0) + jnp.log(1.0 + jnp.exp2(jnp.abs(y) * -LOG2E))


def _sigmoid(y):
    return 1.0 / (1.0 + jnp.exp(-y))


def _params(sem):
    return pltpu.CompilerParams(dimension_semantics=sem, vmem_limit_bytes=VMEM_LIMIT)


def _const_spec(shape):
    nd = len(shape)
    return pl.BlockSpec(shape, lambda *_: (0,) * nd)


def _inproj_body(x_ref, n1_ref, w_ref, qg_ref, kg_ref, gm_ref,
                 q_ref, k_ref, kb_ref, v_ref, vb_ref, pr_ref):
    x = x_ref[...]
    xn = x * lax.rsqrt(jnp.mean(x * x, axis=-1, keepdims=True) + NORM_EPS) * n1_ref[...]
    xb = xn.astype(BF16)

    def head_norm(t, g):
        ms = _dot((t * t).astype(BF16), gm_ref[...])
        return t * lax.rsqrt(ms + NORM_EPS) * g

    q = _dot(xb, w_ref[:, 0:GROUP_W])
    q_ref[...] = (head_norm(q, qg_ref[...]) * SB_SCALE).astype(BF16)
    k = head_norm(_dot(xb, w_ref[:, GROUP_W:2 * GROUP_W]), kg_ref[...])
    kb_ref[...] = k.astype(BF16)
    v = _dot(xb, w_ref[:, 2 * GROUP_W:3 * GROUP_W])
    vb_ref[...] = v.astype(BF16)
    for h in range(N_HEADS):
        k_ref[:, h, :] = k[:, h * HEAD_DIM:(h + 1) * HEAD_DIM]
        v_ref[:, h, :] = v[:, h * HEAD_DIM:(h + 1) * HEAD_DIM]
    pr_ref[...] = _dot(xb, w_ref[:, 3 * GROUP_W:])


def _inproj(x, n1, w_in_b, qg, kg, gmean, tm):
    n, d = x.shape
    pw = w_in_b.shape[1]
    tok = lambda w: pl.BlockSpec((tm, w), lambda i: (i, 0))
    heads = pl.BlockSpec((tm, N_HEADS, HEAD_DIM), lambda i: (i, 0, 0))
    return pl.pallas_call(
        _inproj_body,
        grid=(n // tm,),
        in_specs=[tok(d), _const_spec((1, d)), _const_spec((d, pw)),
                  _const_spec((1, GROUP_W)), _const_spec((1, GROUP_W)),
                  _const_spec((GROUP_W, GROUP_W))],
        out_specs=[tok(GROUP_W), heads, tok(GROUP_W), heads, tok(GROUP_W),
                   tok(RWKV_PROJ_W)],
        out_shape=[jax.ShapeDtypeStruct((n, GROUP_W), BF16),
                   jax.ShapeDtypeStruct((n, N_HEADS, HEAD_DIM), F32),
                   jax.ShapeDtypeStruct((n, GROUP_W), BF16),
                   jax.ShapeDtypeStruct((n, N_HEADS, HEAD_DIM), F32),
                   jax.ShapeDtypeStruct((n, GROUP_W), BF16),
                   jax.ShapeDtypeStruct((n, RWKV_PROJ_W), F32)],
        compiler_params=_params(("parallel",)),
        name="inproj",
    )(x, n1, w_in_b, qg, kg, gmean)


def _sb_weights(z, neg_u, carry, mask):
    sp = _softplus(z)
    if mask is not None:
        sp = jnp.where(mask, sp, 0.0)
    suffix = _dot(sp.astype(BF16), neg_u)
    weight = jnp.exp2(((z - sp) + suffix + carry) * LOG2E)
    if mask is not None:
        weight = jnp.where(mask, weight, 0.0)
    return weight.astype(BF16), carry - jnp.sum(sp, axis=1, keepdims=True)


def _neg_suffix_matrix(tk):
    j = lax.broadcasted_iota(jnp.int32, (tk, tk), 0)
    s = lax.broadcasted_iota(jnp.int32, (tk, tk), 1)
    return jnp.where(j > s, -1.0, 0.0).astype(BF16)


def _sb_prompt_body(bias_ref, q_ref, k_ref, v_ref, u_ref, o_ref,
                    z_scr, w_scr, carry_scr, acc_scr, *, tq):
    hp = pl.program_id(1)
    qi = pl.program_id(2)
    u = u_ref[...]
    q = q_ref[0]
    lane_head = lax.broadcasted_iota(jnp.int32, (tq, LANES), 1) // HEAD_DIM
    zero = jnp.zeros_like(q)
    qs = jnp.concatenate([jnp.where(lane_head == 0, q, zero),
                          jnp.where(lane_head == 1, q, zero)], axis=0)
    bias0 = bias_ref[hp * 2]
    bias1 = bias_ref[hp * 2 + 1]
    qpos = lax.broadcasted_iota(jnp.int32, (tq, tq), 0)
    kpos = lax.broadcasted_iota(jnp.int32, (tq, tq), 1)
    causal = jnp.concatenate([kpos < qpos] * 2, axis=0)

    def logits(kb):
        start = pl.multiple_of(kb * tq, tq)
        zz = _dot_nt(qs, k_ref[0, pl.ds(start, tq), :])
        return jnp.concatenate([zz[:tq] + bias0, zz[tq:] + bias1], axis=0)

    def values(kb, w):
        start = pl.multiple_of(kb * tq, tq)
        return _dot(w, v_ref[0, pl.ds(start, tq), :])

    w, carry = _sb_weights(logits(qi), u, jnp.zeros((2 * tq, 1), F32), causal)
    w_scr[0] = w
    carry_scr[...] = carry
    acc_scr[...] = jnp.zeros((2 * tq, LANES), F32)
    z_scr[1] = logits(jnp.maximum(qi - 1, 0))

    def body(i, _):
        kb = qi - 1 - i
        z = z_scr[(i + 1) % 2]
        z_scr[i % 2] = logits(jnp.maximum(kb - 1, 0))
        acc_scr[...] += values(kb + 1, w_scr[i % 2])
        w, carry = _sb_weights(z, u, carry_scr[...], None)
        w_scr[(i + 1) % 2] = w
        carry_scr[...] = carry
        return 0

    lax.fori_loop(0, qi, body, 0)
    acc = acc_scr[...] + values(0, w_scr[qi % 2])
    o_ref[0] = jnp.where(lane_head == 0, acc[:tq], acc[tq:])


def _sb_prompt(q_b, k_b, v_b, bias, tq):
    b, s, _ = q_b.shape
    hpairs = GROUP_W // LANES
    return pl.pallas_call(
        functools.partial(_sb_prompt_body, tq=tq),
        grid=(b, hpairs, s // tq),
        in_specs=[pl.BlockSpec(memory_space=pltpu.SMEM),
                  pl.BlockSpec((1, tq, LANES), lambda bi, hp, qi: (bi, qi, hp)),
                  pl.BlockSpec((1, s, LANES), lambda bi, hp, qi: (bi, 0, hp)),
                  pl.BlockSpec((1, s, LANES), lambda bi, hp, qi: (bi, 0, hp)),
                  _const_spec((tq, tq))],
        out_specs=pl.BlockSpec((1, tq, LANES), lambda bi, hp, qi: (bi, qi, hp)),
        out_shape=jax.ShapeDtypeStruct((b, s, GROUP_W), F32),
        scratch_shapes=[pltpu.VMEM((2, 2 * tq, tq), F32), pltpu.VMEM((2, 2 * tq, tq), BF16),
                        pltpu.VMEM((2 * tq, 1), F32), pltpu.VMEM((2 * tq, LANES), F32)],
        compiler_params=_params(("parallel", "parallel", "arbitrary")),
        name="sb_prompt",
    )(bias, q_b, k_b, v_b, _neg_suffix_matrix(tq))


def _sb_decode_body(pt_ref, qbd_ref, bias_ref, knew_ref, vnew_ref, u_ref, ck_hbm, cv_hbm, o_ref,
                    acc_ref, carry_ref, kbuf, vbuf, sem, *, pps, t_new, n_pages, side_work=()):
    jj = pl.program_id(1)
    steps_per_seq = pl.num_programs(1)
    step = pl.program_id(0) * steps_per_seq + jj
    n_steps = pl.num_programs(0) * steps_per_seq
    qbd = qbd_ref[0]
    bias = bias_ref[...]
    u = u_ref[...]
    m, page = bias.shape

    def page_copies(s):
        slot = s % PAGE_SLOTS
        first = (s // steps_per_seq) * n_pages + (n_pages - 1) - (s % steps_per_seq) * pps
        copies = []
        for i in range(pps):
            pid = pt_ref[first - i]
            copies.append(pltpu.make_async_copy(ck_hbm.at[pid], kbuf.at[slot, i], sem.at[slot, 0]))
            copies.append(pltpu.make_async_copy(cv_hbm.at[pid], vbuf.at[slot, i], sem.at[slot, 1]))
        return copies

    @pl.when(step == 0)
    def _():
        for s in range(PAGE_SLOTS - 1):
            @pl.when(s < n_steps)
            def _(s=s):
                for c in page_copies(s):
                    c.start()

    @pl.when(step + (PAGE_SLOTS - 1) < n_steps)
    def _():
        for c in page_copies(step + (PAGE_SLOTS - 1)):
            c.start()

    for c in page_copies(step):
        c.wait()
    slot = step % PAGE_SLOTS
    k_refs = [kbuf.at[slot, i] for i in range(pps)]
    v_refs = [vbuf.at[slot, i] for i in range(pps)]

    @pl.when(jj == 0)
    def _():
        pad = jnp.zeros((page - knew_ref.shape[1], GROUP_W), F32)
        kb = jnp.concatenate([knew_ref[0], pad], axis=0).astype(BF16)
        vb = jnp.concatenate([vnew_ref[0], pad], axis=0).astype(BF16)
        qpos = lax.broadcasted_iota(jnp.int32, (m, page), 0) // N_HEADS
        kpos = lax.broadcasted_iota(jnp.int32, (m, page), 1)
        w, carry = _sb_weights(_dot_nt(qbd, kb) + bias, u, jnp.zeros((m, page), F32), kpos < qpos)
        acc_ref[...] = _dot(w, vb)
        carry_ref[...] = carry

    z = jnp.concatenate([_dot(qbd, k_refs[i][...].astype(BF16)) + bias for i in range(pps)], axis=0)
    sp = _softplus(z)
    suffix = _dot(sp.astype(BF16), u)
    sp_sum = jnp.sum(sp, axis=1, keepdims=True)
    carry = carry_ref[...]
    carries = []
    for i in range(pps):
        carries.append(carry)
        carry = carry - sp_sum[i * m:(i + 1) * m]
    carry_ref[...] = carry
    weight = jnp.exp2(((z - sp) + suffix + jnp.concatenate(carries, axis=0)) * LOG2E).astype(BF16)
    acc = acc_ref[...]
    for i in range(pps):
        acc = acc + _dot_nt(weight[i * m:(i + 1) * m], v_refs[i][...].astype(BF16))
    acc_ref[...] = acc
    for piece in side_work:
        piece()

    @pl.when(jj == pl.num_programs(1) - 1)
    def _():
        row_head = lax.broadcasted_iota(jnp.int32, (m, GROUP_W), 0) % N_HEADS
        lane_head = lax.broadcasted_iota(jnp.int32, (m, GROUP_W), 1) // HEAD_DIM
        own = jnp.where(row_head == lane_head, acc, 0.0)
        o_ref[0] = jnp.sum(own.reshape(t_new, N_HEADS, GROUP_W), axis=1)


def _sb_decode(q_b, k_new, v_new, bias, cache_k, cache_v, page_table, pps, rec=None):
    nb, t_new, _ = q_b.shape
    n_pages = page_table.shape[1]
    page = cache_k.shape[2]
    m = t_new * N_HEADS
    t_pad = -(-t_new // 8) * 8
    lane_head = jnp.arange(GROUP_W) // HEAD_DIM
    sel = (lane_head[None, :] == jnp.arange(N_HEADS)[:, None])
    qbd = jnp.where(sel[None, None], q_b[:, :, None, :], jnp.zeros((), BF16)).reshape(nb, m, GROUP_W)
    bias_tile = jnp.broadcast_to(jnp.tile(bias, t_new)[:, None], (m, page)).astype(F32)
    padn = ((0, 0), (0, t_pad - t_new), (0, 0))
    k_new = jnp.pad(k_new, padn)
    v_new = jnp.pad(v_new, padn)

    seq = lambda r, w: pl.BlockSpec((1, r, w), lambda b, jj, pt: (b, 0, 0))
    steps_per_seq = n_pages // pps
    in_specs = [seq(m, GROUP_W),
                pl.BlockSpec((m, page), lambda b, jj, pt: (0, 0)),
                seq(t_pad, GROUP_W), seq(t_pad, GROUP_W),
                pl.BlockSpec((page, page), lambda b, jj, pt: (0, 0)),
                pl.BlockSpec(memory_space=pl.ANY), pl.BlockSpec(memory_space=pl.ANY)]
    out_specs = [seq(t_new, GROUP_W)]
    out_shape = [jax.ShapeDtypeStruct((nb, t_new, GROUP_W), F32)]
    scratch = [pltpu.VMEM((m, GROUP_W), F32), pltpu.VMEM((m, page), F32),
               pltpu.VMEM((PAGE_SLOTS, pps, GROUP_W, page), F32),
               pltpu.VMEM((PAGE_SLOTS, pps, GROUP_W, page), F32),
               pltpu.SemaphoreType.DMA((PAGE_SLOTS, 2))]
    operands = [qbd, bias_tile, k_new, v_new, _neg_suffix_matrix(page), cache_k, cache_v]
    body = functools.partial(_sb_decode_body, pps=pps, t_new=t_new, n_pages=n_pages)
    if rec is not None:
        t_rec, nk, lanes = rec[0].shape
        nv = rec[5].shape[1]
        per_seq = t_rec // nb
        assert nk == HEAD_DIM // 2 and lanes == LANES and t_rec == per_seq * nb
        assert per_seq % steps_per_seq == 0
        steps = lambda rows: pl.BlockSpec((per_seq, rows, LANES), lambda b, jj, pt: (b, 0, 0))
        state = pl.BlockSpec((1, nv // 8, HEAD_DIM, 8, LANES), lambda b, jj, pt: (0, 0, 0, 0, 0))
        in_specs += [steps(nk)] * 5 + [steps(nv), state]
        out_specs += [steps(nv), state]
        out_shape += [jax.ShapeDtypeStruct((t_rec, nv, LANES), F32),
                      jax.ShapeDtypeStruct(rec[6].shape, F32)]
        scratch += [pltpu.VMEM((nv // 8, HEAD_DIM, 8, LANES), F32),
                    pltpu.VMEM((5, per_seq, HEAD_DIM, LANES), F32)]
        operands += list(rec)
        body = functools.partial(_decode_rec_body, pps=pps, t_new=t_new, n_pages=n_pages,
                                 rec_steps=per_seq // steps_per_seq)
    grid_spec = pltpu.PrefetchScalarGridSpec(
        num_scalar_prefetch=1, grid=(nb, steps_per_seq),
        in_specs=in_specs, out_specs=out_specs, scratch_shapes=scratch)
    outs = pl.pallas_call(
        body,
        grid_spec=grid_spec,
        out_shape=out_shape,
        compiler_params=_params(("arbitrary", "arbitrary")),
        name="sb_decode",
    )(page_table.reshape(-1), *operands)
    return outs if rec is not None else outs[0]


def _rwkv_prep_body(pr_ref, sh_ref, mu_ref, w0_ref, a0_ref, w2_ref, a2_ref, g2_ref,
                    kk_ref, ka_ref, rk_ref, gs_ref,
                    w_o, nkk_o, b_o, km_o, r_o, v_o, g_o, bonus_o, prev_ref, *, seq_len):
    pr = pr_ref[0]
    tc = pr.shape[0]
    row = lax.broadcasted_iota(jnp.int32, pr.shape, 0)
    rolled = pltpu.roll(pr, 1, axis=0)
    if seq_len is None:
        @pl.when(pl.program_id(1) == 0)
        def _():
            prev_ref[...] = sh_ref[0]

        prev = jnp.where(row == 0, prev_ref[...], rolled)
        prev_ref[...] = pr[tc - 1:tc, :]
    else:
        prev = jnp.where(row % seq_len == 0, sh_ref[0], rolled)
    xs = pr + (prev - pr) * mu_ref[...]
    r = xs[:, 0:GROUP_W]
    k = xs[:, GROUP_W:2 * GROUP_W]
    v = xs[:, 2 * GROUP_W:3 * GROUP_W]
    wa = xs[:, 3 * GROUP_W:3 * GROUP_W + LANES]
    gd = xs[:, 3 * GROUP_W + LANES:]
    gs = gs_ref[...]
    w_raw = -_softplus(-(w0_ref[...] + _dot(jnp.tanh(wa).astype(BF16), w2_ref[...]))) - 0.5
    w_o[0] = jnp.exp(-jnp.exp(w_raw))
    a = _sigmoid(a0_ref[...] + _dot(wa.astype(BF16), a2_ref[...]))
    g_o[0] = _dot(_sigmoid(gd).astype(BF16), g2_ref[...])
    kk = k * kk_ref[...]
    ss = _dot((kk * kk).astype(BF16), gs)
    kk = kk / jnp.maximum(jnp.sqrt(ss), 1e-12)
    nkk_o[0] = -kk
    b_o[0] = kk * a
    km = k * (1.0 + (a - 1.0) * ka_ref[...])
    km_o[0] = km
    r_o[0] = r
    v_o[0] = v
    bonus_o[0] = _dot((r * km * rk_ref[...]).astype(BF16), gs) * v


def _rwkv_prep(pr, shift, seq_len, mu, w0, a0, w2p, a2p, g2b, k_k, k_a, r_k, gsum, tc):
    b, s, pw = pr.shape
    blk = pl.BlockSpec((1, tc, GROUP_W), lambda bi, c: (bi, c, 0))
    vec = _const_spec((1, GROUP_W))
    if seq_len is None:
        shift_spec = pl.BlockSpec((1, 1, pw), lambda bi, c: (bi, 0, 0))
    else:
        assert tc % seq_len == 0
        shift_spec = pl.BlockSpec((1, tc, pw), lambda bi, c: (bi, c, 0))
    return pl.pallas_call(
        functools.partial(_rwkv_prep_body, seq_len=seq_len),
        grid=(b, s // tc),
        in_specs=[pl.BlockSpec((1, tc, pw), lambda bi, c: (bi, c, 0)),
                  shift_spec,
                  _const_spec((1, pw)), vec, vec,
                  _const_spec((LANES, GROUP_W)), _const_spec((LANES, GROUP_W)),
                  _const_spec((LANES, GROUP_W)), vec, vec, vec,
                  _const_spec((GROUP_W, GROUP_W))],
        out_specs=[blk] * 8,
        out_shape=[jax.ShapeDtypeStruct((b, s, GROUP_W), F32)] * 8,
        scratch_shapes=[pltpu.VMEM((1, pw), F32)],
        compiler_params=_params(("parallel", "arbitrary")),
        name="rwkv_prep",
    )(pr, shift, mu, w0, a0, w2p, a2p, g2b, k_k, k_a, r_k, gsum)


def _spread_keys(x):
    low = lax.broadcasted_iota(jnp.int32, x.shape, 1) < LANES // 2
    other = pltpu.roll(x, LANES // 2, axis=1)
    return jnp.concatenate([jnp.where(low, x, other), jnp.where(low, other, x)], axis=0)


def _rec_step(keyed, t, v_ref, o_ref, s_ref):
    w_ref, nkk_ref, b_ref, km_ref, r_ref = keyed
    nblk, nkeys = s_ref.shape[:2]
    blocks = range(nblk)

    def key_row(ref, k):
        return jnp.broadcast_to(ref[t, pl.ds(k, 1), :], (8, LANES))

    parts = [[None, None] for _ in blocks]
    for k in range(nkeys):
        nkk = key_row(nkk_ref, k)
        for blk in blocks:
            term = s_ref[blk, k] * nkk
            prev = parts[blk][k % 2]
            parts[blk][k % 2] = term if prev is None else prev + term
    sa = [p[0] + p[1] for p in parts]
    v = [v_ref[t, pl.ds(blk * 8, 8), :] for blk in blocks]
    parts = [[None, None] for _ in blocks]
    for k in range(nkeys):
        w, b, km, r = (key_row(ref, k) for ref in (w_ref, b_ref, km_ref, r_ref))
        for blk in blocks:
            s = s_ref[blk, k] * w + sa[blk] * b + v[blk] * km
            s_ref[blk, k] = s
            prev = parts[blk][k % 2]
            parts[blk][k % 2] = s * r if prev is None else prev + s * r
    for blk in blocks:
        o_ref[t, pl.ds(blk * 8, 8), :] = parts[blk][0] + parts[blk][1]


def _rwkv_rec_body(w_ref, nkk_ref, b_ref, km_ref, r_ref, v_ref, s0_ref, o_ref, st_ref, s_ref,
                   *spread_ref, tc):
    c = pl.program_id(1)

    @pl.when(c == 0)
    def _():
        s_ref[...] = s0_ref[0]

    keyed = (w_ref, nkk_ref, b_ref, km_ref, r_ref)
    if spread_ref:
        def spread(i, carry):
            for dt in range(SPREAD_UNROLL):
                t = i * SPREAD_UNROLL + dt
                for j, ref in enumerate(keyed):
                    spread_ref[0][j, t] = _spread_keys(ref[t])
            return carry

        lax.fori_loop(0, tc // SPREAD_UNROLL, spread, 0)
        keyed = tuple(spread_ref[0].at[j] for j in range(len(keyed)))

    def step(t, carry):
        _rec_step(keyed, t, v_ref, o_ref, s_ref)
        return carry

    lax.fori_loop(0, tc, step, 0)

    @pl.when(c == pl.num_programs(1) - 1)
    def _():
        st_ref[0] = s_ref[...]


def _rwkv_rec(w, nkk, b, km, r, v, s0, tc):
    t, nk, l = w.shape
    nv = v.shape[1]
    kvec = pl.BlockSpec((tc, nk, LANES), lambda g, c: (c, 0, g))
    vvec = pl.BlockSpec((tc, nv, LANES), lambda g, c: (c, 0, g))
    st = pl.BlockSpec((1, nv // 8, HEAD_DIM, 8, LANES), lambda g, c: (g, 0, 0, 0, 0))
    scratch = [pltpu.VMEM((nv // 8, HEAD_DIM, 8, LANES), F32)]
    if nk < HEAD_DIM:
        assert tc % SPREAD_UNROLL == 0
        scratch.append(pltpu.VMEM((5, tc, HEAD_DIM, LANES), F32))
    return pl.pallas_call(
        functools.partial(_rwkv_rec_body, tc=tc),
        grid=(l // LANES, t // tc),
        in_specs=[kvec] * 5 + [vvec, st],
        out_specs=[vvec, st],
        out_shape=[jax.ShapeDtypeStruct((t, nv, l), F32),
                   jax.ShapeDtypeStruct(s0.shape, F32)],
        scratch_shapes=scratch,
        compiler_params=_params(("parallel", "arbitrary")),
        name="rwkv_rec",
    )(w, nkk, b, km, r, v, s0)


def _decode_rec_body(pt_ref, qbd_ref, bias_ref, knew_ref, vnew_ref, u_ref, ck_hbm, cv_hbm,
                     w_ref, nkk_ref, b_ref, km_ref, r_ref, v_ref, s0_ref,
                     o_ref, orec_ref, st_ref,
                     acc_ref, carry_ref, kbuf, vbuf, sem, s_ref, spread_ref, *, rec_steps,
                     **decode_args):
    jj = pl.program_id(1)
    step = pl.program_id(0) * pl.num_programs(1) + jj

    @pl.when(step == 0)
    def _():
        s_ref[...] = s0_ref[0]

    @pl.when(jj == 0)
    def _():
        for j, ref in enumerate((w_ref, nkk_ref, b_ref, km_ref, r_ref)):
            for t in range(ref.shape[0]):
                spread_ref[j, t] = _spread_keys(ref[t])

    keyed = tuple(spread_ref.at[j] for j in range(5))
    rec = [functools.partial(_rec_step, keyed, jj * rec_steps + dt, v_ref, orec_ref, s_ref)
           for dt in range(rec_steps)]
    _sb_decode_body(pt_ref, qbd_ref, bias_ref, knew_ref, vnew_ref, u_ref, ck_hbm, cv_hbm, o_ref,
                    acc_ref, carry_ref, kbuf, vbuf, sem, side_work=rec, **decode_args)

    @pl.when(step == pl.num_programs(0) * pl.num_programs(1) - 1)
    def _():
        st_ref[0] = s_ref[...]


def _out_ffn_body(x_ref, att_ref, o_ref, g_ref, bonus_ref, gnw_ref, gnb_ref, gm_ref,
                  wo_ref, n2_ref, wg_ref, wu_ref, wd_ref, y_ref):
    gm = gm_ref[...]
    o = o_ref[...]
    mean = _dot(o.astype(BF16), gm)
    d = o - mean
    var = _dot((d * d).astype(BF16), gm)
    rw = (d * lax.rsqrt(var + GN_EPS) * gnw_ref[...] + gnb_ref[...] + bonus_ref[...]) * g_ref[...]
    h = (x_ref[...] + _dot(att_ref[...].astype(BF16), wo_ref[0:GROUP_W, :])
         + _dot(rw.astype(BF16), wo_ref[GROUP_W:, :]))
    hn = h * lax.rsqrt(jnp.mean(h * h, axis=-1, keepdims=True) + NORM_EPS) * n2_ref[...]
    hb = hn.astype(BF16)
    gate = _dot(hb, wg_ref[...])
    act = gate * _sigmoid(gate) * _dot(hb, wu_ref[...])
    y_ref[...] = h + _dot(act.astype(BF16), wd_ref[...])


def _out_ffn(x, att, o, g, bonus, gn_w, gn_b, gmean, wo_b, n2, wg_b, wu_b, wd_b, tm):
    n, d = x.shape
    dff = wg_b.shape[1]
    tok = lambda w: pl.BlockSpec((tm, w), lambda i: (i, 0))
    once = lambda shape: pl.BlockSpec(shape, lambda i: (0, 0), pipeline_mode=pl.Buffered(1))
    return pl.pallas_call(
        _out_ffn_body,
        grid=(n // tm,),
        in_specs=[tok(d), tok(GROUP_W), tok(GROUP_W), tok(GROUP_W), tok(GROUP_W),
                  once((1, GROUP_W)), once((1, GROUP_W)), once((GROUP_W, GROUP_W)),
                  once((2 * GROUP_W, d)), once((1, d)),
                  once((d, dff)), once((d, dff)), once((dff, d))],
        out_specs=tok(d),
        out_shape=jax.ShapeDtypeStruct((n, d), F32),
        compiler_params=_params(("parallel",)),
        name="out_ffn",
    )(x, att, o, g, bonus, gn_w, gn_b, gmean, wo_b, n2, wg_b, wu_b, wd_b)


def _to_chain_lanes(x, halves):
    b, t, _ = x.shape
    if halves:
        y = x.reshape(b, t, N_HEADS, 2, HEAD_DIM // 2).transpose(1, 4, 3, 0, 2)
        return y.reshape(t, HEAD_DIM // 2, 2 * b * N_HEADS)
    y = x.reshape(b, t, N_HEADS, HEAD_DIM).transpose(1, 3, 2, 0)
    return y.reshape(t, HEAD_DIM, N_HEADS * b)


def _from_chain_lanes(o, b, halves):
    t = o.shape[0]
    if halves:
        y = o.reshape(t, HEAD_DIM // 2, 2, b, N_HEADS).transpose(3, 0, 4, 2, 1)
    else:
        y = o.reshape(t, HEAD_DIM, N_HEADS, b).transpose(3, 0, 2, 1)
    return y.reshape(b, t, GROUP_W)


def _state_to_chain_lanes(s):
    b = s.shape[0]
    y = s.transpose(1, 2, 3, 0)
    if b % LANES == 0:
        y = y.reshape(N_HEADS, HEAD_DIM, HEAD_DIM, b // LANES, LANES).transpose(0, 3, 1, 2, 4)
    else:
        y = y.transpose(1, 2, 0, 3)
    y = y.reshape(N_HEADS * b // LANES, HEAD_DIM // 8, 8, HEAD_DIM, LANES)
    return y.transpose(0, 1, 3, 2, 4)


def _state_from_chain_lanes(s, b, halves):
    s = s.transpose(0, 1, 3, 2, 4)
    if halves:
        y = s.reshape(HEAD_DIM // 2, HEAD_DIM, 2, b, N_HEADS).transpose(3, 4, 2, 0, 1)
    elif b % LANES == 0:
        y = s.reshape(N_HEADS, b // LANES, HEAD_DIM, HEAD_DIM, LANES).transpose(1, 4, 0, 2, 3)
    else:
        y = s.reshape(HEAD_DIM, HEAD_DIM, N_HEADS, b).transpose(3, 2, 0, 1)
    return y.reshape(b, N_HEADS, HEAD_DIM, HEAD_DIM)


def _rwkv_operands(pr, shift_prev, s0, prm, tc_prep):
    b, t, pw = pr.shape
    weights = (prm["mu"], prm["w0"], prm["a0"], prm["w2p"], prm["a2p"], prm["g2b"],
               prm["k_k"], prm["k_a"], prm["r_k"], prm["gsum"])
    if t >= tc_prep:
        vecs = _rwkv_prep(pr, shift_prev[:, None, :], None, *weights, tc_prep)
    else:
        starts = jnp.zeros((b, t, pw), F32).at[:, 0, :].set(shift_prev)
        vecs = _rwkv_prep(pr.reshape(1, b * t, pw), starts.reshape(1, b * t, pw), t,
                          *weights, min(tc_prep, b * t))
        vecs = [a.reshape(b, t, GROUP_W) for a in vecs]
    w, nkk, bv, km, r, v, g, bonus = vecs
    halves = 2 * b * N_HEADS == LANES
    vecs = [_to_chain_lanes(a, halves) for a in (w, nkk, bv, km, r, v)]
    if halves:
        assert s0 is None
        s0c = jnp.zeros((1, HEAD_DIM // 16, HEAD_DIM, 8, LANES), F32)
    else:
        s0c = _state_to_chain_lanes(s0)
    return (*vecs, s0c), g, bonus, halves


def _group_matrix(scale):
    i = jnp.arange(GROUP_W) // HEAD_DIM
    return jnp.where(i[:, None] == i[None, :], scale, 0.0).astype(BF16)


def kernel(x_prompt, x_sample, cache_k, cache_v, state_rwkv, state_shift, page_table, w_in, q_norm, k_norm, sb_bias, mu_shift, w0, w2, a0, a2, g2, k_k, k_a, r_k, gn_w, gn_b, w_o, norm1, norm2, w_gate, w_up, w_down):
    depth = w_in.shape[0]
    bp, sp, d = x_prompt.shape
    bs, ts, _ = x_sample.shape
    gmean = _group_matrix(1.0 / HEAD_DIM)
    gsum = _group_matrix(1.0)
    yp = x_prompt.reshape(bp * sp, d)
    ys = x_sample.reshape(bs * ts, d)
    outs = [[] for _ in range(8)]
    for l in range(depth):
        row = lambda a: a[l][None, :]
        w_in_b = w_in[l].astype(BF16)
        qg = jnp.tile(q_norm[l], N_HEADS)[None, :]
        kg = jnp.tile(k_norm[l], N_HEADS)[None, :]
        lora_pad = jnp.zeros((LANES // 2, GROUP_W), F32)
        prm = dict(mu=row(mu_shift), w0=row(w0), a0=row(a0),
                   w2p=jnp.concatenate([w2[l], lora_pad], axis=0).astype(BF16),
                   a2p=jnp.concatenate([lora_pad, a2[l]], axis=0).astype(BF16),
                   g2b=g2[l].astype(BF16), k_k=row(k_k), k_a=row(k_a), r_k=row(r_k), gsum=gsum)
        ffn = (row(gn_w), row(gn_b), gmean, w_o[l].astype(BF16), row(norm2),
               w_gate[l].astype(BF16), w_up[l].astype(BF16), w_down[l].astype(BF16))

        tm_p = min(TOKEN_TILE, bp * sp)
        tm_s = min(TOKEN_TILE, bs * ts)
        q_p, k_p, kb_p, v_p, vb_p, pr_p = _inproj(yp, row(norm1), w_in_b, qg, kg, gmean, tm_p)
        q_s, k_s, _, v_s, _, pr_s = _inproj(ys, row(norm1), w_in_b, qg, kg, gmean, tm_s)
        pr_p = pr_p.reshape(bp, sp, RWKV_PROJ_W)
        pr_s = pr_s.reshape(bs, ts, RWKV_PROJ_W)
        seq_p = lambda a: a.reshape(bp, sp, GROUP_W)
        seq_s = lambda a: a.reshape(bs, ts, GROUP_W)

        att_p = _sb_prompt(seq_p(q_p), seq_p(kb_p), seq_p(vb_p), sb_bias[l], min(ATT_TILE, sp))
        rec_p, g_p, bonus_p, halves_p = _rwkv_operands(
            pr_p, jnp.zeros((bp, RWKV_PROJ_W), F32), None, prm, PREP_CHUNK)
        rec_s, g_s, bonus_s, halves_s = _rwkv_operands(pr_s, state_shift[l], state_rwkv[l], prm,
                                                       PREP_CHUNK)
        n_pool, page = cache_k.shape[1], cache_k.shape[2]
        pool = lambda c: c[l].transpose(0, 2, 3, 1).reshape(n_pool, GROUP_W, page)
        pps = min(PAGES_PER_STEP, page_table.shape[1])
        decode = functools.partial(_sb_decode, seq_s(q_s), seq_s(k_s), seq_s(v_s), sb_bias[l],
                                   pool(cache_k), pool(cache_v), page_table, pps)
        decode_steps = bs * (page_table.shape[1] // pps)
        if halves_p and sp % decode_steps == 0:
            att_s, o_p, last_p = decode(rec=rec_p)
        else:
            att_s = decode()
            o_p, last_p = _rwkv_rec(*rec_p, min(REC_CHUNK, sp))
        o_s, last_s = _rwkv_rec(*rec_s, min(REC_CHUNK, ts))

        yp = _out_ffn(yp, seq_p(att_p).reshape(bp * sp, GROUP_W),
                      _from_chain_lanes(o_p, bp, halves_p).reshape(bp * sp, GROUP_W),
                      g_p.reshape(bp * sp, GROUP_W), bonus_p.reshape(bp * sp, GROUP_W), *ffn, tm_p)
        ys = _out_ffn(ys, att_s.reshape(bs * ts, GROUP_W),
                      _from_chain_lanes(o_s, bs, halves_s).reshape(bs * ts, GROUP_W),
                      g_s.reshape(bs * ts, GROUP_W), bonus_s.reshape(bs * ts, GROUP_W), *ffn, tm_s)
        outs[0].append(k_p.reshape(bp, sp, N_HEADS, HEAD_DIM))
        outs[1].append(v_p.reshape(bp, sp, N_HEADS, HEAD_DIM))
        outs[2].append(_state_from_chain_lanes(last_p, bp, halves_p))
        outs[3].append(pr_p[:, -1])
        outs[4].append(k_s.reshape(bs, ts, N_HEADS, HEAD_DIM))
        outs[5].append(v_s.reshape(bs, ts, N_HEADS, HEAD_DIM))
        outs[6].append(_state_from_chain_lanes(last_s, bs, halves_s))
        outs[7].append(pr_s[:, -1])
    stacked = [jnp.stack(o) for o in outs]
    return (yp.reshape(bp, sp, d), ys.reshape(bs, ts, d), *stacked)
```

```python
import functools

import jax
import jax.numpy as jnp
from jax import lax
from jax.experimental import pallas as pl
from jax.experimental.pallas import tpu as pltpu

F32 = jnp.float32
BF16 = jnp.bfloat16

HEAD_DIM = 64
N_HEADS = 8
GROUP_W = N_HEADS * HEAD_DIM
LORA_W = 256
RWKV_PROJ_W = 3 * GROUP_W + LORA_W
SB_SCALE = HEAD_DIM ** -0.5
LOG2E = 1.4426950408889634
NORM_EPS = 1e-6
GN_EPS = 64e-5
LANES = 128
VMEM_LIMIT = 56 * 1024 * 1024
TOKEN_TILE = 512
ATT_TILE = 256
PREP_CHUNK = 256
REC_CHUNK = 32
SPREAD_UNROLL = 8
PAGES_PER_STEP = 8
PAGE_SLOTS = 3


def _dot(a, b):
    return jnp.dot(a, b, preferred_element_type=F32)


def _dot_nt(a, b):
    return lax.dot_general(a, b, (((1,), (1,)), ((), ())), preferred_element_type=F32)


def _softplus(y):
    return jnp.maximum(y, 0.0) + jnp.log(1.0 + jnp.exp2(jnp.abs(y) * -LOG2E))


def _sigmoid(y):
    return 1.0 / (1.0 + jnp.exp(-y))


def _params(sem):
    return pltpu.CompilerParams(dimension_semantics=sem, vmem_limit_bytes=VMEM_LIMIT)


def _const_spec(shape):
    nd = len(shape)
    return pl.BlockSpec(shape, lambda *_: (0,) * nd)


def _inproj_body(x_ref, n1_ref, w_ref, qg_ref, kg_ref, gm_ref,
                 q_ref, k_ref, kb_ref, v_ref, vb_ref, pr_ref):
    x = x_ref[...]
    xn = x * lax.rsqrt(jnp.mean(x * x, axis=-1, keepdims=True) + NORM_EPS) * n1_ref[...]
    xb = xn.astype(BF16)

    def head_norm(t, g):
        ms = _dot((t * t).astype(BF16), gm_ref[...])
        return t * lax.rsqrt(ms + NORM_EPS) * g

    q = _dot(xb, w_ref[:, 0:GROUP_W])
    q_ref[...] = (head_norm(q, qg_ref[...]) * SB_SCALE).astype(BF16)
    k = head_norm(_dot(xb, w_ref[:, GROUP_W:2 * GROUP_W]), kg_ref[...])
    kb_ref[...] = k.astype(BF16)
    v = _dot(xb, w_ref[:, 2 * GROUP_W:3 * GROUP_W])
    vb_ref[...] = v.astype(BF16)
    for h in range(N_HEADS):
        k_ref[:, h, :] = k[:, h * HEAD_DIM:(h + 1) * HEAD_DIM]
        v_ref[:, h, :] = v[:, h * HEAD_DIM:(h + 1) * HEAD_DIM]
    pr_ref[...] = _dot(xb, w_ref[:, 3 * GROUP_W:])


def _inproj(x, n1, w_in_b, qg, kg, gmean, tm):
    n, d = x.shape
    pw = w_in_b.shape[1]
    tok = lambda w: pl.BlockSpec((tm, w), lambda i: (i, 0))
    heads = pl.BlockSpec((tm, N_HEADS, HEAD_DIM), lambda i: (i, 0, 0))
    return pl.pallas_call(
        _inproj_body,
        grid=(n // tm,),
        in_specs=[tok(d), _const_spec((1, d)), _const_spec((d, pw)),
                  _const_spec((1, GROUP_W)), _const_spec((1, GROUP_W)),
                  _const_spec((GROUP_W, GROUP_W))],
        out_specs=[tok(GROUP_W), heads, tok(GROUP_W), heads, tok(GROUP_W),
                   tok(RWKV_PROJ_W)],
        out_shape=[jax.ShapeDtypeStruct((n, GROUP_W), BF16),
                   jax.ShapeDtypeStruct((n, N_HEADS, HEAD_DIM), F32),
                   jax.ShapeDtypeStruct((n, GROUP_W), BF16),
                   jax.ShapeDtypeStruct((n, N_HEADS, HEAD_DIM), F32),
                   jax.ShapeDtypeStruct((n, GROUP_W), BF16),
                   jax.ShapeDtypeStruct((n, RWKV_PROJ_W), F32)],
        compiler_params=_params(("parallel",)),
        name="inproj",
    )(x, n1, w_in_b, qg, kg, gmean)


def _sb_weights(z, neg_u, carry, mask):
    sp = _softplus(z)
    if mask is not None:
        sp = jnp.where(mask, sp, 0.0)
    suffix = _dot(sp.astype(BF16), neg_u)
    weight = jnp.exp2(((z - sp) + suffix + carry) * LOG2E)
    if mask is not None:
        weight = jnp.where(mask, weight, 0.0)
    return weight.astype(BF16), carry - jnp.sum(sp, axis=1, keepdims=True)


def _neg_suffix_matrix(tk):
    j = lax.broadcasted_iota(jnp.int32, (tk, tk), 0)
    s = lax.broadcasted_iota(jnp.int32, (tk, tk), 1)
    return jnp.where(j > s, -1.0, 0.0).astype(BF16)


def _sb_prompt_body(bias_ref, q_ref, k_ref, v_ref, u_ref, o_ref,
                    z_scr, w_scr, carry_scr, acc_scr, *, tq):
    hp = pl.program_id(1)
    qi = pl.program_id(2)
    u = u_ref[...]
    q = q_ref[0]
    lane_head = lax.broadcasted_iota(jnp.int32, (tq, LANES), 1) // HEAD_DIM
    zero = jnp.zeros_like(q)
    qs = jnp.concatenate([jnp.where(lane_head == 0, q, zero),
                          jnp.where(lane_head == 1, q, zero)], axis=0)
    bias0 = bias_ref[hp * 2]
    bias1 = bias_ref[hp * 2 + 1]
    qpos = lax.broadcasted_iota(jnp.int32, (tq, tq), 0)
    kpos = lax.broadcasted_iota(jnp.int32, (tq, tq), 1)
    causal = jnp.concatenate([kpos < qpos] * 2, axis=0)

    def logits(kb):
        start = pl.multiple_of(kb * tq, tq)
        zz = _dot_nt(qs, k_ref[0, pl.ds(start, tq), :])
        return jnp.concatenate([zz[:tq] + bias0, zz[tq:] + bias1], axis=0)

    def values(kb, w):
        start = pl.multiple_of(kb * tq, tq)
        return _dot(w, v_ref[0, pl.ds(start, tq), :])

    w, carry = _sb_weights(logits(qi), u, jnp.zeros((2 * tq, 1), F32), causal)
    w_scr[0] = w
    carry_scr[...] = carry
    acc_scr[...] = jnp.zeros((2 * tq, LANES), F32)
    z_scr[1] = logits(jnp.maximum(qi - 1, 0))

    def body(i, _):
        kb = qi - 1 - i
        z = z_scr[(i + 1) % 2]
        z_scr[i % 2] = logits(jnp.maximum(kb - 1, 0))
        acc_scr[...] += values(kb + 1, w_scr[i % 2])
        w, carry = _sb_weights(z, u, carry_scr[...], None)
        w_scr[(i + 1) % 2] = w
        carry_scr[...] = carry
        return 0

    lax.fori_loop(0, qi, body, 0)
    acc = acc_scr[...] + values(0, w_scr[qi % 2])
    o_ref[0] = jnp.where(lane_head == 0, acc[:tq], acc[tq:])


def _sb_prompt(q_b, k_b, v_b, bias, tq):
    b, s, _ = q_b.shape
    hpairs = GROUP_W // LANES
    return pl.pallas_call(
        functools.partial(_sb_prompt_body, tq=tq),
        grid=(b, hpairs, s // tq),
        in_specs=[pl.BlockSpec(memory_space=pltpu.SMEM),
                  pl.BlockSpec((1, tq, LANES), lambda bi, hp, qi: (bi, qi, hp)),
                  pl.BlockSpec((1, s, LANES), lambda bi, hp, qi: (bi, 0, hp)),
                  pl.BlockSpec((1, s, LANES), lambda bi, hp, qi: (bi, 0, hp)),
                  _const_spec((tq, tq))],
        out_specs=pl.BlockSpec((1, tq, LANES), lambda bi, hp, qi: (bi, qi, hp)),
        out_shape=jax.ShapeDtypeStruct((b, s, GROUP_W), F32),
        scratch_shapes=[pltpu.VMEM((2, 2 * tq, tq), F32), pltpu.VMEM((2, 2 * tq, tq), BF16),
                        pltpu.VMEM((2 * tq, 1), F32), pltpu.VMEM((2 * tq, LANES), F32)],
        compiler_params=_params(("parallel", "parallel", "arbitrary")),
        name="sb_prompt",
    )(bias, q_b, k_b, v_b, _neg_suffix_matrix(tq))


def _sb_decode_body(pt_ref, qbd_ref, bias_ref, knew_ref, vnew_ref, u_ref, ck_hbm, cv_hbm, o_ref,
                    acc_ref, carry_ref, kbuf, vbuf, sem, *, pps, t_new, n_pages, side_work=()):
    jj = pl.program_id(1)
    steps_per_seq = pl.num_programs(1)
    step = pl.program_id(0) * steps_per_seq + jj
    n_steps = pl.num_programs(0) * steps_per_seq
    qbd = qbd_ref[0]
    bias = bias_ref[...]
    u = u_ref[...]
    m, page = bias.shape

    def page_copies(s):
        slot = s % PAGE_SLOTS
        first = (s // steps_per_seq) * n_pages + (n_pages - 1) - (s % steps_per_seq) * pps
        copies = []
        for i in range(pps):
            pid = pt_ref[first - i]
            copies.append(pltpu.make_async_copy(ck_hbm.at[pid], kbuf.at[slot, i], sem.at[slot, 0]))
            copies.append(pltpu.make_async_copy(cv_hbm.at[pid], vbuf.at[slot, i], sem.at[slot, 1]))
        return copies

    @pl.when(step == 0)
    def _():
        for s in range(PAGE_SLOTS - 1):
            @pl.when(s < n_steps)
            def _(s=s):
                for c in page_copies(s):
                    c.start()

    @pl.when(step + (PAGE_SLOTS - 1) < n_steps)
    def _():
        for c in page_copies(step + (PAGE_SLOTS - 1)):
            c.start()

    for c in page_copies(step):
        c.wait()
    slot = step % PAGE_SLOTS
    k_refs = [kbuf.at[slot, i] for i in range(pps)]
    v_refs = [vbuf.at[slot, i] for i in range(pps)]

    @pl.when(jj == 0)
    def _():
        pad = jnp.zeros((page - knew_ref.shape[1], GROUP_W), F32)
        kb = jnp.concatenate([knew_ref[0], pad], axis=0).astype(BF16)
        vb = jnp.concatenate([vnew_ref[0], pad], axis=0).astype(BF16)
        qpos = lax.broadcasted_iota(jnp.int32, (m, page), 0) // N_HEADS
        kpos = lax.broadcasted_iota(jnp.int32, (m, page), 1)
        w, carry = _sb_weights(_dot_nt(qbd, kb) + bias, u, jnp.zeros((m, page), F32), kpos < qpos)
        acc_ref[...] = _dot(w, vb)
        carry_ref[...] = carry

    z = jnp.concatenate([_dot(qbd, k_refs[i][...].astype(BF16)) + bias for i in range(pps)], axis=0)
    sp = _softplus(z)
    suffix = _dot(sp.astype(BF16), u)
    sp_sum = jnp.sum(sp, axis=1, keepdims=True)
    carry = carry_ref[...]
    carries = []
    for i in range(pps):
        carries.append(carry)
        carry = carry - sp_sum[i * m:(i + 1) * m]
    carry_ref[...] = carry
    weight = jnp.exp2(((z - sp) + suffix + jnp.concatenate(carries, axis=0)) * LOG2E).astype(BF16)
    acc = acc_ref[...]
    for i in range(pps):
        acc = acc + _dot_nt(weight[i * m:(i + 1) * m], v_refs[i][...].astype(BF16))
    acc_ref[...] = acc
    for piece in side_work:
        piece()

    @pl.when(jj == pl.num_programs(1) - 1)
    def _():
        row_head = lax.broadcasted_iota(jnp.int32, (m, GROUP_W), 0) % N_HEADS
        lane_head = lax.broadcasted_iota(jnp.int32, (m, GROUP_W), 1) // HEAD_DIM
        own = jnp.where(row_head == lane_head, acc, 0.0)
        o_ref[0] = jnp.sum(own.reshape(t_new, N_HEADS, GROUP_W), axis=1)


def _sb_decode(q_b, k_new, v_new, bias, cache_k, cache_v, page_table, pps, rec=None):
    nb, t_new, _ = q_b.shape
    n_pages = page_table.shape[1]
    page = cache_k.shape[2]
    m = t_new * N_HEADS
    t_pad = -(-t_new // 8) * 8
    lane_head = jnp.arange(GROUP_W) // HEAD_DIM
    sel = (lane_head[None, :] == jnp.arange(N_HEADS)[:, None])
    qbd = jnp.where(sel[None, None], q_b[:, :, None, :], jnp.zeros((), BF16)).reshape(nb, m, GROUP_W)
    bias_tile = jnp.broadcast_to(jnp.tile(bias, t_new)[:, None], (m, page)).astype(F32)
    padn = ((0, 0), (0, t_pad - t_new), (0, 0))
    k_new = jnp.pad(k_new, padn)
    v_new = jnp.pad(v_new, padn)

    seq = lambda r, w: pl.BlockSpec((1, r, w), lambda b, jj, pt: (b, 0, 0))
    steps_per_seq = n_pages // pps
    in_specs = [seq(m, GROUP_W),
                pl.BlockSpec((m, page), lambda b, jj, pt: (0, 0)),
                seq(t_pad, GROUP_W), seq(t_pad, GROUP_W),
                pl.BlockSpec((page, page), lambda b, jj, pt: (0, 0)),
                pl.BlockSpec(memory_space=pl.ANY), pl.BlockSpec(memory_space=pl.ANY)]
    out_specs = [seq(t_new, GROUP_W)]
    out_shape = [jax.ShapeDtypeStruct((nb, t_new, GROUP_W), F32)]
    scratch = [pltpu.VMEM((m, GROUP_W), F32), pltpu.VMEM((m, page), F32),
               pltpu.VMEM((PAGE_SLOTS, pps, GROUP_W, page), F32),
               pltpu.VMEM((PAGE_SLOTS, pps, GROUP_W, page), F32),
               pltpu.SemaphoreType.DMA((PAGE_SLOTS, 2))]
    operands = [qbd, bias_tile, k_new, v_new, _neg_suffix_matrix(page), cache_k, cache_v]
    body = functools.partial(_sb_decode_body, pps=pps, t_new=t_new, n_pages=n_pages)
    if rec is not None:
        t_rec, nk, lanes = rec[0].shape
        nv = rec[5].shape[1]
        per_seq = t_rec // nb
        assert nk == HEAD_DIM // 2 and lanes == LANES and t_rec == per_seq * nb
        assert per_seq % steps_per_seq == 0
        steps = lambda rows: pl.BlockSpec((per_seq, rows, LANES), lambda b, jj, pt: (b, 0, 0))
        state = pl.BlockSpec((1, nv // 8, HEAD_DIM, 8, LANES), lambda b, jj, pt: (0, 0, 0, 0, 0))
        in_specs += [steps(nk)] * 5 + [steps(nv), state]
        out_specs += [steps(nv), state]
        out_shape += [jax.ShapeDtypeStruct((t_rec, nv, LANES), F32),
                      jax.ShapeDtypeStruct(rec[6].shape, F32)]
        scratch += [pltpu.VMEM((nv // 8, HEAD_DIM, 8, LANES), F32),
                    pltpu.VMEM((5, per_seq, HEAD_DIM, LANES), F32)]
        operands += list(rec)
        body = functools.partial(_decode_rec_body, pps=pps, t_new=t_new, n_pages=n_pages,
                                 rec_steps=per_seq // steps_per_seq)
    grid_spec = pltpu.PrefetchScalarGridSpec(
        num_scalar_prefetch=1, grid=(nb, steps_per_seq),
        in_specs=in_specs, out_specs=out_specs, scratch_shapes=scratch)
    outs = pl.pallas_call(
        body,
        grid_spec=grid_spec,
        out_shape=out_shape,
        compiler_params=_params(("arbitrary", "arbitrary")),
        name="sb_decode",
    )(page_table.reshape(-1), *operands)
    return outs if rec is not None else outs[0]


def _rwkv_prep_body(pr_ref, sh_ref, mu_ref, w0_ref, a0_ref, w2_ref, a2_ref, g2_ref,
                    kk_ref, ka_ref, rk_ref, gs_ref,
                    w_o, nkk_o, b_o, km_o, r_o, v_o, g_o, bonus_o, prev_ref, *, seq_len):
    pr = pr_ref[0]
    tc = pr.shape[0]
    row = lax.broadcasted_iota(jnp.int32, pr.shape, 0)
    rolled = pltpu.roll(pr, 1, axis=0)
    if seq_len is None:
        @pl.when(pl.program_id(1) == 0)
        def _():
            prev_ref[...] = sh_ref[0]

        prev = jnp.where(row == 0, prev_ref[...], rolled)
        prev_ref[...] = pr[tc - 1:tc, :]
    else:
        prev = jnp.where(row % seq_len == 0, sh_ref[0], rolled)
    xs = pr + (prev - pr) * mu_ref[...]
    r = xs[:, 0:GROUP_W]
    k = xs[:, GROUP_W:2 * GROUP_W]
    v = xs[:, 2 * GROUP_W:3 * GROUP_W]
    wa = xs[:, 3 * GROUP_W:3 * GROUP_W + LANES]
    gd = xs[:, 3 * GROUP_W + LANES:]
    gs = gs_ref[...]
    w_raw = -_softplus(-(w0_ref[...] + _dot(jnp.tanh(wa).astype(BF16), w2_ref[...]))) - 0.5
    w_o[0] = jnp.exp(-jnp.exp(w_raw))
    a = _sigmoid(a0_ref[...] + _dot(wa.astype(BF16), a2_ref[...]))
    g_o[0] = _dot(_sigmoid(gd).astype(BF16), g2_ref[...])
    kk = k * kk_ref[...]
    ss = _dot((kk * kk).astype(BF16), gs)
    kk = kk / jnp.maximum(jnp.sqrt(ss), 1e-12)
    nkk_o[0] = -kk
    b_o[0] = kk * a
    km = k * (1.0 + (a - 1.0) * ka_ref[...])
    km_o[0] = km
    r_o[0] = r
    v_o[0] = v
    bonus_o[0] = _dot((r * km * rk_ref[...]).astype(BF16), gs) * v


def _rwkv_prep(pr, shift, seq_len, mu, w0, a0, w2p, a2p, g2b, k_k, k_a, r_k, gsum, tc):
    b, s, pw = pr.shape
    blk = pl.BlockSpec((1, tc, GROUP_W), lambda bi, c: (bi, c, 0))
    vec = _const_spec((1, GROUP_W))
    if seq_len is None:
        shift_spec = pl.BlockSpec((1, 1, pw), lambda bi, c: (bi, 0, 0))
    else:
        assert tc % seq_len == 0
        shift_spec = pl.BlockSpec((1, tc, pw), lambda bi, c: (bi, c, 0))
    return pl.pallas_call(
        functools.partial(_rwkv_prep_body, seq_len=seq_len),
        grid=(b, s // tc),
        in_specs=[pl.BlockSpec((1, tc, pw), lambda bi, c: (bi, c, 0)),
                  shift_spec,
                  _const_spec((1, pw)), vec, vec,
                  _const_spec((LANES, GROUP_W)), _const_spec((LANES, GROUP_W)),
                  _const_spec((LANES, GROUP_W)), vec, vec, vec,
                  _const_spec((GROUP_W, GROUP_W))],
        out_specs=[blk] * 8,
        out_shape=[jax.ShapeDtypeStruct((b, s, GROUP_W), F32)] * 8,
        scratch_shapes=[pltpu.VMEM((1, pw), F32)],
        compiler_params=_params(("parallel", "arbitrary")),
        name="rwkv_prep",
    )(pr, shift, mu, w0, a0, w2p, a2p, g2b, k_k, k_a, r_k, gsum)


def _spread_keys(x):
    low = lax.broadcasted_iota(jnp.int32, x.shape, 1) < LANES // 2
    other = pltpu.roll(x, LANES // 2, axis=1)
    return jnp.concatenate([jnp.where(low, x, other), jnp.where(low, other, x)], axis=0)


def _rec_step(keyed, t, v_ref, o_ref, s_ref):
    w_ref, nkk_ref, b_ref, km_ref, r_ref = keyed
    nblk, nkeys = s_ref.shape[:2]
    blocks = range(nblk)

    def key_row(ref, k):
        return jnp.broadcast_to(ref[t, pl.ds(k, 1), :], (8, LANES))

    parts = [[None, None] for _ in blocks]
    for k in range(nkeys):
        nkk = key_row(nkk_ref, k)
        for blk in blocks:
            term = s_ref[blk, k] * nkk
            prev = parts[blk][k % 2]
            parts[blk][k % 2] = term if prev is None else prev + term
    sa = [p[0] + p[1] for p in parts]
    v = [v_ref[t, pl.ds(blk * 8, 8), :] for blk in blocks]
    parts = [[None, None] for _ in blocks]
    for k in range(nkeys):
        w, b, km, r = (key_row(ref, k) for ref in (w_ref, b_ref, km_ref, r_ref))
        for blk in blocks:
            s = s_ref[blk, k] * w + sa[blk] * b + v[blk] * km
            s_ref[blk, k] = s
            prev = parts[blk][k % 2]
            parts[blk][k % 2] = s * r if prev is None else prev + s * r
    for blk in blocks:
        o_ref[t, pl.ds(blk * 8, 8), :] = parts[blk][0] + parts[blk][1]


def _rwkv_rec_body(w_ref, nkk_ref, b_ref, km_ref, r_ref, v_ref, s0_ref, o_ref, st_ref, s_ref,
                   *spread_ref, tc):
    c = pl.program_id(1)

    @pl.when(c == 0)
    def _():
        s_ref[...] = s0_ref[0]

    keyed = (w_ref, nkk_ref, b_ref, km_ref, r_ref)
    if spread_ref:
        def spread(i, carry):
            for dt in range(SPREAD_UNROLL):
                t = i * SPREAD_UNROLL + dt
                for j, ref in enumerate(keyed):
                    spread_ref[0][j, t] = _spread_keys(ref[t])
            return carry

        lax.fori_loop(0, tc // SPREAD_UNROLL, spread, 0)
        keyed = tuple(spread_ref[0].at[j] for j in range(len(keyed)))

    def step(t, carry):
        _rec_step(keyed, t, v_ref, o_ref, s_ref)
        return carry

    lax.fori_loop(0, tc, step, 0)

    @pl.when(c == pl.num_programs(1) - 1)
    def _():
        st_ref[0] = s_ref[...]


def _rwkv_rec(w, nkk, b, km, r, v, s0, tc):
    t, nk, l = w.shape
    nv = v.shape[1]
    kvec = pl.BlockSpec((tc, nk, LANES), lambda g, c: (c, 0, g))
    vvec = pl.BlockSpec((tc, nv, LANES), lambda g, c: (c, 0, g))
    st = pl.BlockSpec((1, nv // 8, HEAD_DIM, 8, LANES), lambda g, c: (g, 0, 0, 0, 0))
    scratch = [pltpu.VMEM((nv // 8, HEAD_DIM, 8, LANES), F32)]
    if nk < HEAD_DIM:
        assert tc % SPREAD_UNROLL == 0
        scratch.append(pltpu.VMEM((5, tc, HEAD_DIM, LANES), F32))
    return pl.pallas_call(
        functools.partial(_rwkv_rec_body, tc=tc),
        grid=(l // LANES, t // tc),
        in_specs=[kvec] * 5 + [vvec, st],
        out_specs=[vvec, st],
        out_shape=[jax.ShapeDtypeStruct((t, nv, l), F32),
                   jax.ShapeDtypeStruct(s0.shape, F32)],
        scratch_shapes=scratch,
        compiler_params=_params(("parallel", "arbitrary")),
        name="rwkv_rec",
    )(w, nkk, b, km, r, v, s0)


def _decode_rec_body(pt_ref, qbd_ref, bias_ref, knew_ref, vnew_ref, u_ref, ck_hbm, cv_hbm,
                     w_ref, nkk_ref, b_ref, km_ref, r_ref, v_ref, s0_ref,
                     o_ref, orec_ref, st_ref,
                     acc_ref, carry_ref, kbuf, vbuf, sem, s_ref, spread_ref, *, rec_steps,
                     **decode_args):
    jj = pl.program_id(1)
    step = pl.program_id(0) * pl.num_programs(1) + jj

    @pl.when(step == 0)
    def _():
        s_ref[...] = s0_ref[0]

    @pl.when(jj == 0)
    def _():
        for j, ref in enumerate((w_ref, nkk_ref, b_ref, km_ref, r_ref)):
            for t in range(ref.shape[0]):
                spread_ref[j, t] = _spread_keys(ref[t])

    keyed = tuple(spread_ref.at[j] for j in range(5))
    rec = [functools.partial(_rec_step, keyed, jj * rec_steps + dt, v_ref, orec_ref, s_ref)
           for dt in range(rec_steps)]
    _sb_decode_body(pt_ref, qbd_ref, bias_ref, knew_ref, vnew_ref, u_ref, ck_hbm, cv_hbm, o_ref,
                    acc_ref, carry_ref, kbuf, vbuf, sem, side_work=rec, **decode_args)

    @pl.when(step == pl.num_programs(0) * pl.num_programs(1) - 1)
    def _():
        st_ref[0] = s_ref[...]


def _out_ffn_body(x_ref, att_ref, o_ref, g_ref, bonus_ref, gnw_ref, gnb_ref, gm_ref,
                  wo_ref, n2_ref, wg_ref, wu_ref, wd_ref, y_ref):
    gm = gm_ref[...]
    o = o_ref[...]
    mean = _dot(o.astype(BF16), gm)
    d = o - mean
    var = _dot((d * d).astype(BF16), gm)
    rw = (d * lax.rsqrt(var + GN_EPS) * gnw_ref[...] + gnb_ref[...] + bonus_ref[...]) * g_ref[...]
    h = (x_ref[...] + _dot(att_ref[...].astype(BF16), wo_ref[0:GROUP_W, :])
         + _dot(rw.astype(BF16), wo_ref[GROUP_W:, :]))
    hn = h * lax.rsqrt(jnp.mean(h * h, axis=-1, keepdims=True) + NORM_EPS) * n2_ref[...]
    hb = hn.astype(BF16)
    gate = _dot(hb, wg_ref[...])
    act = gate * _sigmoid(gate) * _dot(hb, wu_ref[...])
    y_ref[...] = h + _dot(act.astype(BF16), wd_ref[...])


def _out_ffn(x, att, o, g, bonus, gn_w, gn_b, gmean, wo_b, n2, wg_b, wu_b, wd_b, tm):
    n, d = x.shape
    dff = wg_b.shape[1]
    tok = lambda w: pl.BlockSpec((tm, w), lambda i: (i, 0))
    once = lambda shape: pl.BlockSpec(shape, lambda i: (0, 0), pipeline_mode=pl.Buffered(1))
    return pl.pallas_call(
        _out_ffn_body,
        grid=(n // tm,),
        in_specs=[tok(d), tok(GROUP_W), tok(GROUP_W), tok(GROUP_W), tok(GROUP_W),
                  once((1, GROUP_W)), once((1, GROUP_W)), once((GROUP_W, GROUP_W)),
                  once((2 * GROUP_W, d)), once((1, d)),
                  once((d, dff)), once((d, dff)), once((dff, d))],
        out_specs=tok(d),
        out_shape=jax.ShapeDtypeStruct((n, d), F32),
        compiler_params=_params(("parallel",)),
        name="out_ffn",
    )(x, att, o, g, bonus, gn_w, gn_b, gmean, wo_b, n2, wg_b, wu_b, wd_b)


def _to_chain_lanes(x, halves):
    b, t, _ = x.shape
    if halves:
        y = x.reshape(b, t, N_HEADS, 2, HEAD_DIM // 2).transpose(1, 4, 3, 0, 2)
        return y.reshape(t, HEAD_DIM // 2, 2 * b * N_HEADS)
    y = x.reshape(b, t, N_HEADS, HEAD_DIM).transpose(1, 3, 2, 0)
    return y.reshape(t, HEAD_DIM, N_HEADS * b)


def _from_chain_lanes(o, b, halves):
    t = o.shape[0]
    if halves:
        y = o.reshape(t, HEAD_DIM // 2, 2, b, N_HEADS).transpose(3, 0, 4, 2, 1)
    else:
        y = o.reshape(t, HEAD_DIM, N_HEADS, b).transpose(3, 0, 2, 1)
    return y.reshape(b, t, GROUP_W)


def _state_to_chain_lanes(s):
    b = s.shape[0]
    y = s.transpose(1, 2, 3, 0)
    if b % LANES == 0:
        y = y.reshape(N_HEADS, HEAD_DIM, HEAD_DIM, b // LANES, LANES).transpose(0, 3, 1, 2, 4)
    else:
        y = y.transpose(1, 2, 0, 3)
    y = y.reshape(N_HEADS * b // LANES, HEAD_DIM // 8, 8, HEAD_DIM, LANES)
    return y.transpose(0, 1, 3, 2, 4)


def _state_from_chain_lanes(s, b, halves):
    s = s.transpose(0, 1, 3, 2, 4)
    if halves:
        y = s.reshape(HEAD_DIM // 2, HEAD_DIM, 2, b, N_HEADS).transpose(3, 4, 2, 0, 1)
    elif b % LANES == 0:
        y = s.reshape(N_HEADS, b // LANES, HEAD_DIM, HEAD_DIM, LANES).transpose(1, 4, 0, 2, 3)
    else:
        y = s.reshape(HEAD_DIM, HEAD_DIM, N_HEADS, b).transpose(3, 2, 0, 1)
    return y.reshape(b, N_HEADS, HEAD_DIM, HEAD_DIM)


def _rwkv_operands(pr, shift_prev, s0, prm, tc_prep):
    b, t, pw = pr.shape
    weights = (prm["mu"], prm["w0"], prm["a0"], prm["w2p"], prm["a2p"], prm["g2b"],
               prm["k_k"], prm["k_a"], prm["r_k"], prm["gsum"])
    if t >= tc_prep:
        vecs = _rwkv_prep(pr, shift_prev[:, None, :], None, *weights, tc_prep)
    else:
        starts = jnp.zeros((b, t, pw), F32).at[:, 0, :].set(shift_prev)
        vecs = _rwkv_prep(pr.reshape(1, b * t, pw), starts.reshape(1, b * t, pw), t,
                          *weights, min(tc_prep, b * t))
        vecs = [a.reshape(b, t, GROUP_W) for a in vecs]
    w, nkk, bv, km, r, v, g, bonus = vecs
    halves = 2 * b * N_HEADS == LANES
    vecs = [_to_chain_lanes(a, halves) for a in (w, nkk, bv, km, r, v)]
    if halves:
        assert s0 is None
        s0c = jnp.zeros((1, HEAD_DIM // 16, HEAD_DIM, 8, LANES), F32)
    else:
        s0c = _state_to_chain_lanes(s0)
    return (*vecs, s0c), g, bonus, halves


def _group_matrix(scale):
    i = jnp.arange(GROUP_W) // HEAD_DIM
    return jnp.where(i[:, None] == i[None, :], scale, 0.0).astype(BF16)


def kernel(x_prompt, x_sample, cache_k, cache_v, state_rwkv, state_shift, page_table, w_in, q_norm, k_norm, sb_bias, mu_shift, w0, w2, a0, a2, g2, k_k, k_a, r_k, gn_w, gn_b, w_o, norm1, norm2, w_gate, w_up, w_down):
    depth = w_in.shape[0]
    bp, sp, d = x_prompt.shape
    bs, ts, _ = x_sample.shape
    gmean = _group_matrix(1.0 / HEAD_DIM)
    gsum = _group_matrix(1.0)
    yp = x_prompt.reshape(bp * sp, d)
    ys = x_sample.reshape(bs * ts, d)
    outs = [[] for _ in range(8)]
    for l in range(depth):
        row = lambda a: a[l][None, :]
        w_in_b = w_in[l].astype(BF16)
        qg = jnp.tile(q_norm[l], N_HEADS)[None, :]
        kg = jnp.tile(k_norm[l], N_HEADS)[None, :]
        lora_pad = jnp.zeros((LANES // 2, GROUP_W), F32)
        prm = dict(mu=row(mu_shift), w0=row(w0), a0=row(a0),
                   w2p=jnp.concatenate([w2[l], lora_pad], axis=0).astype(BF16),
                   a2p=jnp.concatenate([lora_pad, a2[l]], axis=0).astype(BF16),
                   g2b=g2[l].astype(BF16), k_k=row(k_k), k_a=row(k_a), r_k=row(r_k), gsum=gsum)
        ffn = (row(gn_w), row(gn_b), gmean, w_o[l].astype(BF16), row(norm2),
               w_gate[l].astype(BF16), w_up[l].astype(BF16), w_down[l].astype(BF16))

        tm_p = min(TOKEN_TILE, bp * sp)
        tm_s = min(TOKEN_TILE, bs * ts)
        q_p, k_p, kb_p, v_p, vb_p, pr_p = _inproj(yp, row(norm1), w_in_b, qg, kg, gmean, tm_p)
        q_s, k_s, _, v_s, _, pr_s = _inproj(ys, row(norm1), w_in_b, qg, kg, gmean, tm_s)
        pr_p = pr_p.reshape(bp, sp, RWKV_PROJ_W)
        pr_s = pr_s.reshape(bs, ts, RWKV_PROJ_W)
        seq_p = lambda a: a.reshape(bp, sp, GROUP_W)
        seq_s = lambda a: a.reshape(bs, ts, GROUP_W)

        att_p = _sb_prompt(seq_p(q_p), seq_p(kb_p), seq_p(vb_p), sb_bias[l], min(ATT_TILE, sp))
        rec_p, g_p, bonus_p, halves_p = _rwkv_operands(
            pr_p, jnp.zeros((bp, RWKV_PROJ_W), F32), None, prm, PREP_CHUNK)
        rec_s, g_s, bonus_s, halves_s = _rwkv_operands(pr_s, state_shift[l], state_rwkv[l], prm,
                                                       PREP_CHUNK)
        n_pool, page = cache_k.shape[1], cache_k.shape[2]
        pool = lambda c: c[l].transpose(0, 2, 3, 1).reshape(n_pool, GROUP_W, page)
        pps = min(PAGES_PER_STEP, page_table.shape[1])
        decode = functools.partial(_sb_decode, seq_s(q_s), seq_s(k_s), seq_s(v_s), sb_bias[l],
                                   pool(cache_k), pool(cache_v), page_table, pps)
        decode_steps = bs * (page_table.shape[1] // pps)
        if halves_p and sp % decode_steps == 0:
            att_s, o_p, last_p = decode(rec=rec_p)
        else:
            att_s = decode()
            o_p, last_p = _rwkv_rec(*rec_p, min(REC_CHUNK, sp))
        o_s, last_s = _rwkv_rec(*rec_s, min(REC_CHUNK, ts))

        yp = _out_ffn(yp, seq_p(att_p).reshape(bp * sp, GROUP_W),
                      _from_chain_lanes(o_p, bp, halves_p).reshape(bp * sp, GROUP_W),
                      g_p.reshape(bp * sp, GROUP_W), bonus_p.reshape(bp * sp, GROUP_W), *ffn, tm_p)
        ys = _out_ffn(ys, att_s.reshape(bs * ts, GROUP_W),
                      _from_chain_lanes(o_s, bs, halves_s).reshape(bs * ts, GROUP_W),
                      g_s.reshape(bs * ts, GROUP_W), bonus_s.reshape(bs * ts, GROUP_W), *ffn, tm_s)
        outs[0].append(k_p.reshape(bp, sp, N_HEADS, HEAD_DIM))
        outs[1].append(v_p.reshape(bp, sp, N_HEADS, HEAD_DIM))
        outs[2].append(_state_from_chain_lanes(last_p, bp, halves_p))
        outs[3].append(pr_p[:, -1])
        outs[4].append(k_s.reshape(bs, ts, N_HEADS, HEAD_DIM))
        outs[5].append(v_s.reshape(bs, ts, N_HEADS, HEAD_DIM))
        outs[6].append(_state_from_chain_lanes(last_s, bs, halves_s))
        outs[7].append(pr_s[:, -1])
    stacked = [jnp.stack(o) for o in outs]
    return (yp.reshape(bp, sp, d), ys.reshape(bs, ts, d), *stacked)
```
